```python
import math
import jax, jax.numpy as jnp
from jax import lax
import numpy as np

D_MODEL = 1024
BATCH = 16
SEQ = 4096
DEPTH = 1

SSD_HEADS = 16
SSD_HEAD_DIM = 64
SSD_INNER = SSD_HEADS * SSD_HEAD_DIM
SSD_GROUPS = 2
SSD_STATE = 128
SSD_BC = SSD_GROUPS * SSD_STATE
SSD_CONV = 4
SSD_CONV_DIM = SSD_INNER + 2 * SSD_BC
SSD_CHUNK = 128
DT_MIN = 0.001
DT_MAX = 0.1
ATT_HEADS = 16
ATT_KV_HEADS = 4
ATT_HEAD_DIM = 64
ATT_INNER = ATT_HEADS * ATT_HEAD_DIM
IDX_HEADS = 8
IDX_DIM = 64
TOPK_MAX = 256
Q_BLOCK = 128
MIX_WIDTH = SSD_INNER + ATT_INNER
IN_SPLITS = (SSD_INNER, SSD_CONV_DIM, SSD_HEADS,
             ATT_INNER, ATT_KV_HEADS * ATT_HEAD_DIM, ATT_KV_HEADS * ATT_HEAD_DIM,
             IDX_HEADS * IDX_DIM, IDX_DIM, IDX_HEADS)
D_IN_PROJ = sum(IN_SPLITS)
D_FF = 2816
FFN_CONV = 3
LN_EPS = 1e-5
RMS_EPS = 1e-5
DEEPNORM_ALPHA = (2 * DEPTH) ** 0.25
DEEPNORM_BETA = (8 * DEPTH) ** -0.25

kernel_name = "hymba_ssd_dsa_convffn_deepnorm"


def _layer_norm(x, g, b):
    xf = x.astype(jnp.float32)
    mu = jnp.mean(xf, axis=-1, keepdims=True)
    var = jnp.mean(jnp.square(xf - mu), axis=-1, keepdims=True)
    return ((xf - mu) * lax.rsqrt(var + LN_EPS) * g + b).astype(x.dtype)


def _causal_dwconv(x, w, b):
    width = w.shape[0]
    y = lax.conv_general_dilated(
        x, w[:, None, :].astype(x.dtype), window_strides=(1,),
        padding=[(width - 1, 0)], dimension_numbers=("NWC", "WIO", "NWC"),
        feature_group_count=x.shape[-1])
    return y + b


def _ssd_scan(xs, dt, a, bm, cm):
    bsz, seqlen, nh, hp = xs.shape
    q = SSD_CHUNK
    nc = seqlen // q
    hg = nh // SSD_GROUPS
    xdt = (xs.astype(jnp.float32) * dt[..., None]).reshape(bsz, nc, q, SSD_GROUPS, hg, hp)
    acs = jnp.cumsum((dt * a).reshape(bsz, nc, q, SSD_GROUPS, hg), axis=2)
    bc = bm.astype(jnp.float32).reshape(bsz, nc, q, SSD_GROUPS, SSD_STATE)
    cc = cm.astype(jnp.float32).reshape(bsz, nc, q, SSD_GROUPS, SSD_STATE)
    causal = jnp.tril(jnp.ones((q, q), dtype=bool))
    seg = acs[:, :, :, None] - acs[:, :, None, :]
    lmat = jnp.exp(jnp.where(causal[:, :, None, None], seg, -jnp.inf))
    cb = jnp.einsum("bclgn,bcsgn->bclsg", cc, bc)
    y_diag = jnp.einsum("bclsg,bclsgh,bcsghp->bclghp", cb, lmat, xdt)
    decay_to_end = jnp.exp(acs[:, :, -1:] - acs)
    states = jnp.einsum("bclgn,bclgh,bclghp->bcghpn", bc, decay_to_end, xdt)
    chunk_decay = jnp.exp(acs[:, :, -1])

    def step(carry, inp):
        st, dec = inp
        return carry * dec[..., None, None] + st, carry

    init = jnp.zeros((bsz, SSD_GROUPS, hg, hp, SSD_STATE), jnp.float32)
    _, prev = lax.scan(step, init, (jnp.moveaxis(states, 1, 0), jnp.moveaxis(chunk_decay, 1, 0)))
    prev = jnp.moveaxis(prev, 0, 1)
    y_off = jnp.einsum("bclgn,bcghpn,bclgh->bclghp", cc, prev, jnp.exp(acs))
    return (y_diag + y_off).reshape(bsz, seqlen, nh, hp)


def _dsa_attention(q, k, v, q_idx, k_idx, w_idx):
    bsz, seqlen = q.shape[:2]
    topk = min(TOPK_MAX, seqlen // 4)
    nb = seqlen // Q_BLOCK
    n_rep = ATT_HEADS // ATT_KV_HEADS
    key_pos = jnp.arange(seqlen)
    gather = jax.vmap(lambda t, idx: t[idx])

    def to_blocks(t):
        return jnp.moveaxis(t.reshape(bsz, nb, Q_BLOCK, *t.shape[2:]), 1, 0)

    def block(args):
        qb, qib, wb, start = args
        q_pos = start + jnp.arange(Q_BLOCK)
        visible = key_pos[None, :] <= q_pos[:, None]
        s = jnp.einsum("bthd,bsd->bths", qib, k_idx).astype(jnp.float32) * IDX_DIM ** -0.5
        score = jnp.einsum("bths,bth->bts", jax.nn.relu(s), wb.astype(jnp.float32))
        score = jnp.where(visible[None], score, -jnp.inf)
        _, top_idx = lax.top_k(score, topk)
        keep = top_idx <= q_pos[None, :, None]
        k_sel = gather(k, top_idx)
        v_sel = gather(v, top_idx)
        qg = qb.reshape(bsz, Q_BLOCK, ATT_KV_HEADS, n_rep, ATT_HEAD_DIM)
        logits = jnp.einsum("btngd,btsnd->btngs", qg, k_sel).astype(jnp.float32) * ATT_HEAD_DIM ** -0.5
        logits = jnp.where(keep[:, :, None, None, :], logits, -jnp.inf)
        p = jax.nn.softmax(logits, axis=-1).astype(v.dtype)
        o = jnp.einsum("btngs,btsnd->btngd", p, v_sel)
        return o.reshape(bsz, Q_BLOCK, ATT_INNER)

    starts = jnp.arange(nb, dtype=jnp.int32) * Q_BLOCK
    out = lax.map(block, (to_blocks(q), to_blocks(q_idx), to_blocks(w_idx), starts))
    return jnp.moveaxis(out, 0, 1).reshape(bsz, seqlen, ATT_INNER)


def _hybrid_mixer(x, w_in, ssd_conv_w, ssd_conv_b, dt_bias, a_log, d_skip, ssd_norm_g,
                  idx_k_norm_g, idx_k_norm_b, w_out):
    bsz, seqlen, _ = x.shape
    proj = x @ w_in
    offsets = np.cumsum(IN_SPLITS)[:-1].tolist()
    z, xbc, dt_raw, q, k, v, qi, ki, wi = jnp.split(proj, offsets, axis=-1)
    xbc = jax.nn.silu(_causal_dwconv(xbc, ssd_conv_w, ssd_conv_b))
    xs, bm, cm = jnp.split(xbc, [SSD_INNER, SSD_INNER + SSD_BC], axis=-1)
    xs = xs.reshape(bsz, seqlen, SSD_HEADS, SSD_HEAD_DIM)
    dt = jax.nn.softplus(dt_raw.astype(jnp.float32) + dt_bias.astype(jnp.float32))
    a = -jnp.exp(a_log.astype(jnp.float32))
    y = _ssd_scan(xs, dt, a,
                  bm.reshape(bsz, seqlen, SSD_GROUPS, SSD_STATE),
                  cm.reshape(bsz, seqlen, SSD_GROUPS, SSD_STATE))
    y = y + d_skip.astype(jnp.float32)[:, None] * xs.astype(jnp.float32)
    gy = (y.reshape(bsz, seqlen, SSD_INNER) * jax.nn.silu(z.astype(jnp.float32)))
    gy = gy.reshape(bsz, seqlen, SSD_GROUPS, SSD_INNER // SSD_GROUPS)
    gy = gy * lax.rsqrt(jnp.mean(jnp.square(gy), axis=-1, keepdims=True) + RMS_EPS)
    y_ssd = (gy.reshape(bsz, seqlen, SSD_INNER) * ssd_norm_g).astype(x.dtype)
    q = q.reshape(bsz, seqlen, ATT_HEADS, ATT_HEAD_DIM)
    k = k.reshape(bsz, seqlen, ATT_KV_HEADS, ATT_HEAD_DIM)
    v = v.reshape(bsz, seqlen, ATT_KV_HEADS, ATT_HEAD_DIM)
    qi = qi.reshape(bsz, seqlen, IDX_HEADS, IDX_DIM)
    ki = _layer_norm(ki, idx_k_norm_g, idx_k_norm_b)
    wi = wi * IDX_HEADS ** -0.5
    y_att = _dsa_attention(q, k, v, qi, ki, wi)
    return jnp.concatenate([y_ssd, y_att.astype(x.dtype)], axis=-1) @ w_out


def _conv_ffn(h, w_up, conv_w, conv_b, w_down):
    u = _causal_dwconv(h @ w_up, conv_w, conv_b)
    gate, up = jnp.split(u, 2, axis=-1)
    return (jax.nn.silu(gate) * up) @ w_down


def setup_inputs(seed: int = 0) -> dict:
    key = jax.random.key(seed)
    ks = jax.random.split(key, 20)
    f32 = jnp.float32

    def nrm(k, shape, scale):
        return jax.random.normal(k, shape, f32) * scale

    x = nrm(ks[0], (BATCH, SEQ, D_MODEL), 1.0)
    v_off = sum(IN_SPLITS[:5])
    col_scale = jnp.ones((D_IN_PROJ,), f32).at[v_off:v_off + IN_SPLITS[5]].set(DEEPNORM_BETA)
    w_in = nrm(ks[1], (DEPTH, D_MODEL, D_IN_PROJ), D_MODEL ** -0.5) * col_scale
    ssd_conv_w = nrm(ks[2], (DEPTH, SSD_CONV, SSD_CONV_DIM), SSD_CONV ** -0.5)
    ssd_conv_b = nrm(ks[3], (DEPTH, SSD_CONV_DIM), 0.02)
    dt0 = jnp.exp(jax.random.uniform(ks[4], (DEPTH, SSD_HEADS), f32, math.log(DT_MIN), math.log(DT_MAX)))
    dt_bias = dt0 + jnp.log(-jnp.expm1(-dt0))
    a_log = jnp.log(jax.random.uniform(ks[5], (DEPTH, SSD_HEADS), f32, 1.0, 16.0))
    d_skip = 1.0 + nrm(ks[6], (DEPTH, SSD_HEADS), 0.1)
    ssd_norm_g = 1.0 + nrm(ks[7], (DEPTH, SSD_INNER), 0.02)
    idx_k_norm_g = 1.0 + nrm(ks[8], (DEPTH, IDX_DIM), 0.02)
    idx_k_norm_b = nrm(ks[9], (DEPTH, IDX_DIM), 0.02)
    w_out = nrm(ks[10], (DEPTH, MIX_WIDTH, D_MODEL), MIX_WIDTH ** -0.5 * DEEPNORM_BETA)
    ln1_g = 1.0 + nrm(ks[11], (DEPTH, D_MODEL), 0.02)
    ln1_b = nrm(ks[12], (DEPTH, D_MODEL), 0.02)
    ffn_w_up = nrm(ks[13], (DEPTH, D_MODEL, 2 * D_FF), D_MODEL ** -0.5)
    ffn_conv_w = nrm(ks[14], (DEPTH, FFN_CONV, 2 * D_FF), FFN_CONV ** -0.5)
    ffn_conv_b = nrm(ks[15], (DEPTH, 2 * D_FF), 0.02)
    ffn_w_down = nrm(ks[16], (DEPTH, D_FF, D_MODEL), D_FF ** -0.5 * DEEPNORM_BETA)
    ln2_g = 1.0 + nrm(ks[17], (DEPTH, D_MODEL), 0.02)
    ln2_b = nrm(ks[18], (DEPTH, D_MODEL), 0.02)
    return {"x": x, "w_in": w_in, "ssd_conv_w": ssd_conv_w, "ssd_conv_b": ssd_conv_b,
            "dt_bias": dt_bias, "a_log": a_log, "d_skip": d_skip, "ssd_norm_g": ssd_norm_g,
            "idx_k_norm_g": idx_k_norm_g, "idx_k_norm_b": idx_k_norm_b, "w_out": w_out,
            "ln1_g": ln1_g, "ln1_b": ln1_b, "ffn_w_up": ffn_w_up, "ffn_conv_w": ffn_conv_w,
            "ffn_conv_b": ffn_conv_b, "ffn_w_down": ffn_w_down, "ln2_g": ln2_g, "ln2_b": ln2_b}


def reference(x, w_in, ssd_conv_w, ssd_conv_b, dt_bias, a_log, d_skip, ssd_norm_g,
              idx_k_norm_g, idx_k_norm_b, w_out, ln1_g, ln1_b, ffn_w_up, ffn_conv_w,
              ffn_conv_b, ffn_w_down, ln2_g, ln2_b):
    for i in range(DEPTH):
        m = _hybrid_mixer(x, w_in[i], ssd_conv_w[i], ssd_conv_b[i], dt_bias[i], a_log[i],
                          d_skip[i], ssd_norm_g[i], idx_k_norm_g[i], idx_k_norm_b[i], w_out[i])
        h = _layer_norm(DEEPNORM_ALPHA * x + m, ln1_g[i], ln1_b[i])
        f = _conv_ffn(h, ffn_w_up[i], ffn_conv_w[i], ffn_conv_b[i], ffn_w_down[i])
        x = _layer_norm(DEEPNORM_ALPHA * h + f, ln2_g[i], ln2_b[i])
    return x
```

```python
import functools

import jax
import jax.numpy as jnp
from jax import lax
from jax.experimental import pallas as pl
from jax.experimental.pallas import tpu as pltpu

F32 = jnp.float32
BF16 = jnp.bfloat16
I32 = jnp.int32
HIGHEST = lax.Precision.HIGHEST

SSD_HEADS = 16
SSD_HEAD_DIM = 64
SSD_INNER = SSD_HEADS * SSD_HEAD_DIM
SSD_GROUPS = 2
SSD_STATE = 128
SSD_BC = SSD_GROUPS * SSD_STATE
SSD_CONV = 4
SSD_CONV_DIM = SSD_INNER + 2 * SSD_BC
SSD_CHUNK = 128
GROUP_W = SSD_INNER // SSD_GROUPS
ATT_HEADS = 16
ATT_KV_HEADS = 4
ATT_HEAD_DIM = 64
ATT_INNER = ATT_HEADS * ATT_HEAD_DIM
ATT_KV_W = ATT_KV_HEADS * ATT_HEAD_DIM
N_REP = ATT_HEADS // ATT_KV_HEADS
IDX_HEADS = 8
IDX_DIM = 64
TOPK_MAX = 256
Q_BLOCK = 128
FFN_CONV = 3
LN_EPS = 1e-5
RMS_EPS = 1e-5
DEPTH = 1
DEEPNORM_ALPHA = (2 * DEPTH) ** 0.25

LANES = 128
VMEM_LIMIT = 48 * 1024 * 1024
KEY_TILE = 512
MASK_VALUE = -1e30
INT_MIN = -(2 ** 31)

PF_Z_BLK = SSD_CONV_DIM // GROUP_W
PF_DT_BLK = (SSD_CONV_DIM + SSD_INNER) // LANES
PF_IDX_BLK = PF_DT_BLK + 1
PF_WIDTH = SSD_CONV_DIM + SSD_INNER + 2 * LANES
PB_K_BLK = ATT_INNER // ATT_KV_W
PB_V_BLK = PB_K_BLK + 1
PB_QI_BLK = (ATT_INNER + 2 * ATT_KV_W) // (IDX_HEADS * IDX_DIM)
PB_WIDTH = ATT_INNER + 2 * ATT_KV_W + IDX_HEADS * IDX_DIM


def _layer_norm_rows(r, g, b):
    mu = jnp.mean(r, axis=-1, keepdims=True)
    d = r - mu
    var = jnp.mean(d * d, axis=-1, keepdims=True)
    return d * lax.rsqrt(var + LN_EPS) * g + b


def _silu(x):
    return x * jax.nn.sigmoid(x)


def _matmul_kernel(x_ref, w_ref, o_ref, xb_ref):
    @pl.when(pl.program_id(1) == 0)
    def _cast():
        xb_ref[...] = x_ref[...].astype(BF16)

    o_ref[...] = jnp.dot(xb_ref[...], w_ref[...], preferred_element_type=F32).astype(o_ref.dtype)


def _project(x2d, w, out_dtype, tm, tn):
    n, k = x2d.shape
    width = w.shape[1]
    return pl.pallas_call(
        _matmul_kernel,
        out_shape=jax.ShapeDtypeStruct((n, width), out_dtype),
        grid=(n // tm, width // tn),
        in_specs=[pl.BlockSpec((tm, k), lambda i, j: (i, 0)),
                  pl.BlockSpec((k, tn), lambda i, j: (0, j))],
        out_specs=pl.BlockSpec((tm, tn), lambda i, j: (i, j)),
        scratch_shapes=[pltpu.VMEM((tm, k), BF16)],
        compiler_params=pltpu.CompilerParams(
            dimension_semantics=("parallel", "arbitrary"), vmem_limit_bytes=VMEM_LIMIT),
        name="in_proj",
    )(x2d, w)


def _ssd_kernel(xbc_ref, z0_ref, z1_ref, dt_ref, cw_ref, cb_ref, dtb_ref, alog_ref, dsk_ref,
                g_ref, e_ref, tril_ref, y_ref, xpad_ref, st_ref):
    q = SSD_CHUNK
    halo = 8

    @pl.when(pl.program_id(1) == 0)
    def _init():
        xpad_ref[0:halo, :] = jnp.zeros((halo, SSD_CONV_DIM), F32)
        st_ref[...] = jnp.zeros_like(st_ref)

    xpad_ref[halo:halo + q, :] = xbc_ref[...]
    conv = cb_ref[...] + cw_ref[0:1, :] * xpad_ref[halo - 3:halo - 3 + q, :]
    for kk in range(1, SSD_CONV):
        conv = conv + cw_ref[kk:kk + 1, :] * xpad_ref[halo - 3 + kk:halo - 3 + kk + q, :]
    xpad_ref[0:halo, :] = xpad_ref[q:q + halo, :]
    xbc = _silu(conv)
    xs = xbc[:, :SSD_INNER]
    bm = xbc[:, SSD_INNER:SSD_INNER + SSD_BC]
    cm = xbc[:, SSD_INNER + SSD_BC:]

    dt_in = dt_ref[...] + dtb_ref[...]
    dtv = jnp.maximum(dt_in, 0.0) + jnp.log1p(jnp.exp(-jnp.abs(dt_in)))
    da = dtv * (-jnp.exp(alog_ref[...]))
    tril = tril_ref[...]
    acs = jnp.dot(tril, da, precision=HIGHEST, preferred_element_type=F32)
    acs_t = acs.T
    acs_last = acs[q - 1:q, :]
    dte = jnp.exp(acs_last - acs)
    eacs = jnp.exp(acs)
    cdec = jnp.broadcast_to(jnp.exp(acs_last), (8, LANES))
    stacked = jnp.concatenate([dtv, dte, eacs, cdec], axis=0)
    expanded = jnp.dot(stacked, e_ref[...], precision=HIGHEST, preferred_element_type=F32)
    dt_x = expanded[0:q]
    dte_x = expanded[q:2 * q]
    eacs_x = expanded[2 * q:3 * q]
    cdec_x = expanded[3 * q:3 * q + 1]

    xdt = xs * dt_x
    xdt_b = xdt.astype(BF16)
    wst_b = (xdt * dte_x).astype(BF16)
    causal = tril > 0.5
    lane = lax.broadcasted_iota(I32, (q, LANES), 1)
    lo_half = lane < SSD_HEAD_DIM
    heads_per_group = SSD_HEADS // SSD_GROUPS
    zs = (z0_ref, z1_ref)

    for g in range(SSD_GROUPS):
        gs = slice(g * GROUP_W, (g + 1) * GROUP_W)
        b_g = bm[:, g * SSD_STATE:(g + 1) * SSD_STATE]
        c_gb = cm[:, g * SSD_STATE:(g + 1) * SSD_STATE].astype(BF16)
        b_gb = b_g.astype(BF16)
        cb = lax.dot_general(c_gb, b_gb, (((1,), (1,)), ((), ())), preferred_element_type=F32)
        prev = st_ref[g]
        y_off = jnp.dot(c_gb, prev.astype(BF16), preferred_element_type=F32) * eacs_x[:, gs]
        st_ref[g] = prev * cdec_x[:, gs] + jnp.dot(b_g.T.astype(BF16), wst_b[:, gs],
                                                   preferred_element_type=F32)
        pairs = []
        for j in range(heads_per_group // 2):
            ms = []
            for h in (g * heads_per_group + 2 * j, g * heads_per_group + 2 * j + 1):
                seg = acs[:, h:h + 1] - acs_t[h:h + 1, :]
                lmat = jnp.where(causal, jnp.exp(seg), 0.0)
                ms.append((cb * lmat).astype(BF16))
            c0 = g * GROUP_W + 2 * j * SSD_HEAD_DIM
            xp = xdt_b[:, c0:c0 + LANES]
            zero = jnp.zeros_like(xp)
            rhs = jnp.concatenate([jnp.where(lo_half, xp, zero), jnp.where(lo_half, zero, xp)], axis=0)
            pairs.append(jnp.dot(jnp.concatenate(ms, axis=1), rhs, preferred_element_type=F32))
        y = jnp.concatenate(pairs, axis=1) + y_off + dsk_ref[:, gs] * xs[:, gs]
        gy = y * _silu(zs[g][...])
        ms2 = jnp.mean(gy * gy, axis=-1, keepdims=True)
        y_ref[:, gs] = (gy * lax.rsqrt(ms2 + RMS_EPS) * g_ref[:, gs]).astype(y_ref.dtype)


def _ssd(pf, bsz, seqlen, conv_w, conv_b, dt_bias, a_log, d_skip, norm_g):
    nc = seqlen // SSD_CHUNK
    q = SSD_CHUNK
    pad = LANES - SSD_HEADS
    dtb = jnp.pad(dt_bias.astype(F32), (0, pad)).reshape(1, LANES)
    alog = jnp.pad(a_log.astype(F32), (0, pad)).reshape(1, LANES)
    dsk = jnp.repeat(d_skip.astype(F32), SSD_HEAD_DIM).reshape(1, SSD_INNER)
    expand = (jnp.arange(LANES)[:, None] == (jnp.arange(SSD_INNER)[None, :] // SSD_HEAD_DIM)).astype(F32)
    tril = jnp.tril(jnp.ones((q, q), F32))
    row = lambda b, c: b * nc + c
    const = lambda b, c: (0, 0)
    return pl.pallas_call(
        _ssd_kernel,
        out_shape=jax.ShapeDtypeStruct((bsz * seqlen, SSD_INNER), BF16),
        grid=(bsz, nc),
        in_specs=[
            pl.BlockSpec((q, SSD_CONV_DIM), lambda b, c: (row(b, c), 0)),
            pl.BlockSpec((q, GROUP_W), lambda b, c: (row(b, c), PF_Z_BLK)),
            pl.BlockSpec((q, GROUP_W), lambda b, c: (row(b, c), PF_Z_BLK + 1)),
            pl.BlockSpec((q, LANES), lambda b, c: (row(b, c), PF_DT_BLK)),
            pl.BlockSpec((SSD_CONV, SSD_CONV_DIM), const),
            pl.BlockSpec((1, SSD_CONV_DIM), const),
            pl.BlockSpec((1, LANES), const),
            pl.BlockSpec((1, LANES), const),
            pl.BlockSpec((1, SSD_INNER), const),
            pl.BlockSpec((1, SSD_INNER), const),
            pl.BlockSpec((LANES, SSD_INNER), const),
            pl.BlockSpec((q, q), const),
        ],
        out_specs=pl.BlockSpec((q, SSD_INNER), lambda b, c: (row(b, c), 0)),
        scratch_shapes=[pltpu.VMEM((q + 8, SSD_CONV_DIM), F32),
                        pltpu.VMEM((SSD_GROUPS, SSD_STATE, GROUP_W), F32)],
        compiler_params=pltpu.CompilerParams(
            dimension_semantics=("parallel", "arbitrary"), vmem_limit_bytes=VMEM_LIMIT),
        name="ssd",
    )(pf, pf, pf, pf, conv_w.astype(F32), conv_b.astype(F32).reshape(1, -1), dtb, alog, dsk,
      norm_g.astype(F32).reshape(1, -1), expand, tril)


def _dsa_kernel(q_ref, k_ref, v_ref, qi_ref, idx_all_ref, idx_q_ref, ng_ref, nb_ref, o_ref,
                ki_s, key_s, q4_s, m_s, l_s, acc_s, *, seqlen, topk):
    tk = min(KEY_TILE, seqlen)
    qb = pl.program_id(1)
    n_kt = (qb * Q_BLOCK + Q_BLOCK + tk - 1) // tk
    qpos = qb * Q_BLOCK + lax.broadcasted_iota(I32, (Q_BLOCK, 1), 0)

    def key_cols(kt):
        return pl.ds(pl.multiple_of(kt * tk, tk), tk)

    def key_pos(kt):
        return kt * tk + lax.broadcasted_iota(I32, (1, tk), 1)

    @pl.when(qb == 0)
    def _prep():
        def body(i, carry):
            rows = pl.ds(pl.multiple_of(i * tk, tk), tk)
            kin = idx_all_ref[rows, :][:, :IDX_DIM]
            ki_s[rows, :] = _layer_norm_rows(kin, ng_ref[...], nb_ref[...]).astype(BF16)
            return carry
        lax.fori_loop(0, seqlen // tk, body, 0)

    w_all = idx_q_ref[...][:, IDX_DIM:IDX_DIM + IDX_HEADS] * (IDX_HEADS ** -0.5) * (IDX_DIM ** -0.5)
    w_cols = [w_all[:, h:h + 1] for h in range(IDX_HEADS)]
    qi = qi_ref[...]
    qi_heads = [qi[:, h * IDX_DIM:(h + 1) * IDX_DIM] for h in range(IDX_HEADS)]

    def score_tile(kt, carry):
        kit = ki_s[key_cols(kt), :]
        acc = jnp.zeros((Q_BLOCK, tk), F32)
        for h in range(IDX_HEADS):
            s = lax.dot_general(qi_heads[h], kit, (((1,), (1,)), ((), ())), preferred_element_type=F32)
            acc = acc + jnp.maximum(s, 0.0) * w_cols[h]
        bits = pltpu.bitcast(acc, I32)
        keys = bits ^ ((bits >> 31) & jnp.int32(0x7FFFFFFF))
        keys = jnp.where(acc == 0.0, 0, keys)
        key_s[:, key_cols(kt)] = jnp.where(key_pos(kt) <= qpos, keys, INT_MIN)
        return carry

    lax.fori_loop(0, n_kt, score_tile, 0)

    def count_rows(pred):
        def body(kt, cnt):
            m = pred(key_s[:, key_cols(kt)], kt).astype(I32)
            for c in range(tk // LANES):
                cnt = cnt + m[:, c * LANES:(c + 1) * LANES]
            return cnt
        cnt = lax.fori_loop(0, n_kt, body, jnp.zeros((Q_BLOCK, LANES), I32))
        return jnp.sum(cnt, axis=1, keepdims=True)

    def bit_step(i, t_u):
        cand_u = t_u | jnp.left_shift(jnp.int32(1), 31 - i)
        cand_s = cand_u ^ INT_MIN
        cnt = count_rows(lambda keys, kt: keys >= cand_s)
        return jnp.where(cnt >= topk, cand_u, t_u)

    thr = lax.fori_loop(0, 32, bit_step, jnp.zeros((Q_BLOCK, 1), I32)) ^ INT_MIN
    n_gt = count_rows(lambda keys, kt: keys > thr)
    n_eq = count_rows(lambda keys, kt: keys == thr)
    need = topk - n_gt

    def tie_cut():
        def idx_step(i, p):
            cand = p | jnp.left_shift(jnp.int32(1), (seqlen - 1).bit_length() - 1 - i)
            cnt = count_rows(lambda keys, kt: (keys == thr) & (key_pos(kt) < cand))
            return jnp.where(cnt < need, cand, p)
        return lax.fori_loop(0, (seqlen - 1).bit_length(), idx_step, jnp.zeros((Q_BLOCK, 1), I32))

    excess = jnp.max(jnp.where((n_eq > need) & (thr != INT_MIN), 1, 0))
    cut = lax.cond(excess > 0, tie_cut, lambda: jnp.full((Q_BLOCK, 1), seqlen, I32))

    qs = q_ref[...] * jnp.asarray(ATT_HEAD_DIM ** -0.5, BF16)
    for n in range(ATT_KV_HEADS):
        q4_s[n] = jnp.concatenate(
            [qs[:, (n * N_REP + g) * ATT_HEAD_DIM:(n * N_REP + g + 1) * ATT_HEAD_DIM] for g in range(N_REP)],
            axis=0)
    m_s[...] = jnp.full(m_s.shape, MASK_VALUE, F32)
    l_s[...] = jnp.zeros(l_s.shape, F32)
    acc_s[...] = jnp.zeros(acc_s.shape, F32)

    def attn_tile(kt, carry):
        keys = key_s[:, key_cols(kt)]
        kpos = key_pos(kt)
        sel = ((keys > thr) | ((keys == thr) & (kpos <= cut))) & (kpos <= qpos)
        bias = jnp.where(sel, 0.0, MASK_VALUE)
        bias4 = jnp.concatenate([bias] * N_REP, axis=0)
        for n in range(ATT_KV_HEADS):
            k_n = k_ref[key_cols(kt), n * ATT_HEAD_DIM:(n + 1) * ATT_HEAD_DIM]
            v_n = v_ref[key_cols(kt), n * ATT_HEAD_DIM:(n + 1) * ATT_HEAD_DIM]
            s = lax.dot_general(q4_s[n], k_n, (((1,), (1,)), ((), ())), preferred_element_type=F32) + bias4
            m_prev = m_s[n]
            m_new = jnp.maximum(m_prev, jnp.max(s, axis=1, keepdims=True))
            alpha = jnp.exp(m_prev - m_new)
            p = jnp.exp(s - m_new)
            l_s[n] = alpha * l_s[n] + jnp.sum(p, axis=1, keepdims=True)
            acc_s[n] = alpha * acc_s[n] + jnp.dot(p.astype(BF16), v_n, preferred_element_type=F32)
            m_s[n] = m_new
        return carry

    lax.fori_loop(0, n_kt, attn_tile, 0)

    for n in range(ATT_KV_HEADS):
        o_n = acc_s[n] / l_s[n]
        for gp in range(N_REP // 2):
            pair = jnp.concatenate([o_n[(2 * gp) * Q_BLOCK:(2 * gp + 1) * Q_BLOCK],
                                    o_n[(2 * gp + 1) * Q_BLOCK:(2 * gp + 2) * Q_BLOCK]], axis=1)
            c0 = (n * N_REP + 2 * gp) * ATT_HEAD_DIM
            o_ref[:, c0:c0 + LANES] = pair.astype(o_ref.dtype)


def _dsa(pb, pf, bsz, seqlen, norm_g, norm_b):
    nb = seqlen // Q_BLOCK
    topk = min(TOPK_MAX, seqlen // 4)
    row = lambda b, i: b * nb + i
    const = lambda b, i: (0, 0)
    rows4 = N_REP * Q_BLOCK
    return pl.pallas_call(
        functools.partial(_dsa_kernel, seqlen=seqlen, topk=topk),
        out_shape=jax.ShapeDtypeStruct((bsz * seqlen, ATT_INNER), BF16),
        grid=(bsz, nb),
        in_specs=[
            pl.BlockSpec((Q_BLOCK, ATT_INNER), lambda b, i: (row(b, i), 0)),
            pl.BlockSpec((seqlen, ATT_KV_W), lambda b, i: (b, PB_K_BLK)),
            pl.BlockSpec((seqlen, ATT_KV_W), lambda b, i: (b, PB_V_BLK)),
            pl.BlockSpec((Q_BLOCK, IDX_HEADS * IDX_DIM), lambda b, i: (row(b, i), PB_QI_BLK)),
            pl.BlockSpec((seqlen, LANES), lambda b, i: (b, PF_IDX_BLK)),
            pl.BlockSpec((Q_BLOCK, LANES), lambda b, i: (row(b, i), PF_IDX_BLK)),
            pl.BlockSpec((1, IDX_DIM), const),
            pl.BlockSpec((1, IDX_DIM), const),
        ],
        out_specs=pl.BlockSpec((Q_BLOCK, ATT_INNER), lambda b, i: (row(b, i), 0)),
        scratch_shapes=[
            pltpu.VMEM((seqlen, IDX_DIM), BF16),
            pltpu.VMEM((Q_BLOCK, seqlen), I32),
            pltpu.VMEM((ATT_KV_HEADS, rows4, ATT_HEAD_DIM), BF16),
            pltpu.VMEM((ATT_KV_HEADS, rows4, 1), F32),
            pltpu.VMEM((ATT_KV_HEADS, rows4, 1), F32),
            pltpu.VMEM((ATT_KV_HEADS, rows4, ATT_HEAD_DIM), F32),
        ],
        compiler_params=pltpu.CompilerParams(
            dimension_semantics=("parallel", "arbitrary"), vmem_limit_bytes=VMEM_LIMIT),
        name="dsa",
    )(pb, pb, pb, pb, pf, pf, norm_g.astype(F32).reshape(1, -1), norm_b.astype(F32).reshape(1, -1))


def _outproj_kernel(ys_ref, ya_ref, x_ref, w1_ref, w2_ref, g_ref, b_ref, h_ref):
    m = jnp.dot(ys_ref[...], w1_ref[...], preferred_element_type=F32)
    m = m + jnp.dot(ya_ref[...], w2_ref[...], preferred_element_type=F32)
    h_ref[...] = _layer_norm_rows(DEEPNORM_ALPHA * x_ref[...] + m, g_ref[...], b_ref[...])


def _out_proj(y_ssd, y_att, x2d, w_out, ln_g, ln_b, tm):
    n, d = x2d.shape
    w1 = w_out[:SSD_INNER].astype(BF16)
    w2 = w_out[SSD_INNER:].astype(BF16)
    rows = lambda i: (i, 0)
    const = lambda i: (0, 0)
    return pl.pallas_call(
        _outproj_kernel,
        out_shape=jax.ShapeDtypeStruct((n, d), F32),
        grid=(n // tm,),
        in_specs=[pl.BlockSpec((tm, SSD_INNER), rows), pl.BlockSpec((tm, ATT_INNER), rows),
                  pl.BlockSpec((tm, d), rows),
                  pl.BlockSpec((SSD_INNER, d), const), pl.BlockSpec((ATT_INNER, d), const),
                  pl.BlockSpec((1, d), const), pl.BlockSpec((1, d), const)],
        out_specs=pl.BlockSpec((tm, d), rows),
        compiler_params=pltpu.CompilerParams(
            dimension_semantics=("parallel",), vmem_limit_bytes=VMEM_LIMIT),
        name="out_proj_ln",
    )(y_ssd, y_att, x2d, w1, w2, ln_g.astype(F32).reshape(1, -1), ln_b.astype(F32).reshape(1, -1))


FFN_HALO = 16


def _ffn_kernel(h_ref, hprev_ref, wg_ref, wu_ref, cwg_ref, cwu_ref, cbg_ref, cbu_ref, wd_ref,
                g_ref, b_ref, o_ref, hb_ref, acc_ref, *, tiles_per_seq):
    i = pl.program_id(0)
    j = pl.program_id(1)
    tm = h_ref.shape[0]

    @pl.when(j == 0)
    def _stage():
        starts_sequence = (i % tiles_per_seq) == 0
        hb_ref[0:FFN_HALO, :] = jnp.where(starts_sequence, 0.0, hprev_ref[...]).astype(BF16)
        hb_ref[FFN_HALO:, :] = h_ref[...].astype(BF16)
        acc_ref[...] = jnp.zeros_like(acc_ref)

    hb = hb_ref[...]

    def conv_branch(w_ref, cw_ref, cb_ref):
        u = jnp.dot(hb, w_ref[...], preferred_element_type=F32)
        c = cb_ref[...] + cw_ref[FFN_CONV - 1:FFN_CONV, :] * u
        for back in range(1, FFN_CONV):
            c = c + cw_ref[FFN_CONV - 1 - back:FFN_CONV - back, :] * pltpu.roll(u, back, 0)
        return c[FFN_HALO:]

    act = _silu(conv_branch(wg_ref, cwg_ref, cbg_ref)) * conv_branch(wu_ref, cwu_ref, cbu_ref)
    acc_ref[...] += jnp.dot(act.astype(BF16), wd_ref[...], preferred_element_type=F32)

    @pl.when(j == pl.num_programs(1) - 1)
    def _finish():
        o_ref[...] = _layer_norm_rows(DEEPNORM_ALPHA * h_ref[...] + acc_ref[...], g_ref[...], b_ref[...])


def _conv_ffn(h, seqlen, w_up, conv_w, conv_b, w_down, ln_g, ln_b, tm, n_split):
    n, d = h.shape
    d_ff = w_down.shape[0]
    tf = d_ff // n_split
    halo_blocks = tm // FFN_HALO
    w_up_b = w_up.astype(BF16)
    conv_b2 = conv_b.astype(F32).reshape(1, -1)
    rows = lambda i, j: (i, 0)
    const = lambda i, j: (0, 0)
    gate = lambda i, j: (0, j)
    up = lambda i, j: (0, n_split + j)
    return pl.pallas_call(
        functools.partial(_ffn_kernel, tiles_per_seq=seqlen // tm),
        out_shape=jax.ShapeDtypeStruct((n, d), F32),
        grid=(n // tm, n_split),
        in_specs=[pl.BlockSpec((tm, d), rows),
                  pl.BlockSpec((FFN_HALO, d), lambda i, j: (jnp.maximum(i * halo_blocks - 1, 0), 0)),
                  pl.BlockSpec((d, tf), gate), pl.BlockSpec((d, tf), up),
                  pl.BlockSpec((FFN_CONV, tf), gate), pl.BlockSpec((FFN_CONV, tf), up),
                  pl.BlockSpec((1, tf), gate), pl.BlockSpec((1, tf), up),
                  pl.BlockSpec((tf, d), lambda i, j: (j, 0)),
                  pl.BlockSpec((1, d), const), pl.BlockSpec((1, d), const)],
        out_specs=pl.BlockSpec((tm, d), rows),
        scratch_shapes=[pltpu.VMEM((tm + FFN_HALO, d), BF16), pltpu.VMEM((tm, d), F32)],
        compiler_params=pltpu.CompilerParams(
            dimension_semantics=("parallel", "arbitrary"), vmem_limit_bytes=VMEM_LIMIT),
        name="conv_ffn_ln",
    )(h, h, w_up_b, w_up_b, conv_w.astype(F32), conv_w.astype(F32), conv_b2, conv_b2,
      w_down.astype(BF16), ln_g.astype(F32).reshape(1, -1), ln_b.astype(F32).reshape(1, -1))


def _split_w_in(w_in):
    o = 0
    parts = {}
    for name, width in (("z", SSD_INNER), ("xbc", SSD_CONV_DIM), ("dt", SSD_HEADS), ("q", ATT_INNER),
                        ("k", ATT_KV_W), ("v", ATT_KV_W), ("qi", IDX_HEADS * IDX_DIM),
                        ("ki", IDX_DIM), ("wi", IDX_HEADS)):
        parts[name] = w_in[:, o:o + width]
        o += width
    d = w_in.shape[0]
    zeros = lambda w: jnp.zeros((d, w), w_in.dtype)
    wf = jnp.concatenate([parts["xbc"], parts["z"], parts["dt"], zeros(LANES - SSD_HEADS),
                          parts["ki"], parts["wi"], zeros(LANES - IDX_DIM - IDX_HEADS)], axis=1)
    wb = jnp.concatenate([parts["q"], parts["k"], parts["v"], parts["qi"]], axis=1)
    return wf.astype(BF16), wb.astype(BF16)


def _row_tile(n, seqlen, want):
    tm = min(want, seqlen)
    assert n % tm == 0 and seqlen % tm == 0
    return tm


def kernel(x, w_in, ssd_conv_w, ssd_conv_b, dt_bias, a_log, d_skip, ssd_norm_g, idx_k_norm_g,
           idx_k_norm_b, w_out, ln1_g, ln1_b, ffn_w_up, ffn_conv_w, ffn_conv_b, ffn_w_down, ln2_g, ln2_b):
    bsz, seqlen, d = x.shape
    n = bsz * seqlen
    assert seqlen % SSD_CHUNK == 0 and seqlen % Q_BLOCK == 0 and seqlen % min(KEY_TILE, seqlen) == 0
    h = x.reshape(n, d)
    for i in range(DEPTH):
        wf, wb = _split_w_in(w_in[i])
        tm = _row_tile(n, seqlen, 1024)
        pf = _project(h, wf, F32, tm, PF_WIDTH // 2)
        pb = _project(h, wb, BF16, tm, PB_WIDTH // 2)
        y_ssd = _ssd(pf, bsz, seqlen, ssd_conv_w[i], ssd_conv_b[i], dt_bias[i], a_log[i], d_skip[i],
                     ssd_norm_g[i])
        y_att = _dsa(pb, pf, bsz, seqlen, idx_k_norm_g[i], idx_k_norm_b[i])
        h1 = _out_proj(y_ssd, y_att, h, w_out[i], ln1_g[i], ln1_b[i], _row_tile(n, seqlen, 512))
        h = _conv_ffn(h1, seqlen, ffn_w_up[i], ffn_conv_w[i], ffn_conv_b[i], ffn_w_down[i],
                      ln2_g[i], ln2_b[i], _row_tile(n, seqlen, 512), 2)
    return h.reshape(bsz, seqlen, d)
```

```python
import functools

import jax
import jax.numpy as jnp
from jax import lax
from jax.experimental import pallas as pl
from jax.experimental.pallas import tpu as pltpu

F32 = jnp.float32
BF16 = jnp.bfloat16
I32 = jnp.int32
HIGHEST = lax.Precision.HIGHEST

SSD_HEADS = 16
SSD_HEAD_DIM = 64
SSD_INNER = SSD_HEADS * SSD_HEAD_DIM
SSD_GROUPS = 2
SSD_STATE = 128
SSD_BC = SSD_GROUPS * SSD_STATE
SSD_CONV = 4
SSD_CONV_DIM = SSD_INNER + 2 * SSD_BC
SSD_CHUNK = 128
GROUP_W = SSD_INNER // SSD_GROUPS
ATT_HEADS = 16
ATT_KV_HEADS = 4
ATT_HEAD_DIM = 64
ATT_INNER = ATT_HEADS * ATT_HEAD_DIM
ATT_KV_W = ATT_KV_HEADS * ATT_HEAD_DIM
N_REP = ATT_HEADS // ATT_KV_HEADS
IDX_HEADS = 8
IDX_DIM = 64
TOPK_MAX = 256
Q_BLOCK = 128
FFN_CONV = 3
LN_EPS = 1e-5
RMS_EPS = 1e-5
DEPTH = 1
DEEPNORM_ALPHA = (2 * DEPTH) ** 0.25

LANES = 128
VMEM_LIMIT = 48 * 1024 * 1024
KEY_TILE = 512
MASK_VALUE = -1e30
INT_MIN = -(2 ** 31)
V_ONES_ROWS = 16

PF_Z_BLK = SSD_CONV_DIM // GROUP_W
PF_DT_BLK = (SSD_CONV_DIM + SSD_INNER) // LANES
PF_IDX_BLK = PF_DT_BLK + 1
PF_WIDTH = SSD_CONV_DIM + SSD_INNER + 2 * LANES
PB_K_BLK = ATT_INNER // ATT_KV_W
PB_V_BLK = PB_K_BLK + 1
PB_QI_BLK = (ATT_INNER + 2 * ATT_KV_W) // (IDX_HEADS * IDX_DIM)
PB_WIDTH = ATT_INNER + 2 * ATT_KV_W + IDX_HEADS * IDX_DIM


def _layer_norm_rows(r, g, b):
    mu = jnp.mean(r, axis=-1, keepdims=True)
    d = r - mu
    var = jnp.mean(d * d, axis=-1, keepdims=True)
    return d * lax.rsqrt(var + LN_EPS) * g + b


def _silu(x):
    return x * jax.nn.sigmoid(x)


def _matmul_kernel(x_ref, w_ref, o_ref, xb_ref):
    @pl.when(pl.program_id(1) == 0)
    def _cast():
        xb_ref[...] = x_ref[...].astype(BF16)

    o_ref[...] = jnp.dot(xb_ref[...], w_ref[...], preferred_element_type=F32).astype(o_ref.dtype)


def _project(x2d, w, out_dtype, tm, tn):
    n, k = x2d.shape
    width = w.shape[1]
    return pl.pallas_call(
        _matmul_kernel,
        out_shape=jax.ShapeDtypeStruct((n, width), out_dtype),
        grid=(n // tm, width // tn),
        in_specs=[pl.BlockSpec((tm, k), lambda i, j: (i, 0)),
                  pl.BlockSpec((k, tn), lambda i, j: (0, j))],
        out_specs=pl.BlockSpec((tm, tn), lambda i, j: (i, j)),
        scratch_shapes=[pltpu.VMEM((tm, k), BF16)],
        compiler_params=pltpu.CompilerParams(
            dimension_semantics=("parallel", "arbitrary"), vmem_limit_bytes=VMEM_LIMIT),
        name="in_proj",
    )(x2d, w)


def _ssd_kernel(xbc_ref, z0_ref, z1_ref, dt_ref, cw_ref, cb_ref, dtb_ref, alog_ref, dsk_ref,
                g_ref, e_ref, tril_ref, y_ref, xpad_ref, st_ref):
    q = SSD_CHUNK
    halo = 8

    @pl.when(pl.program_id(1) == 0)
    def _init():
        xpad_ref[0:halo, :] = jnp.zeros((halo, SSD_CONV_DIM), F32)
        st_ref[...] = jnp.zeros_like(st_ref)

    xpad_ref[halo:halo + q, :] = xbc_ref[...]
    conv = cb_ref[...] + cw_ref[0:1, :] * xpad_ref[halo - 3:halo - 3 + q, :]
    for kk in range(1, SSD_CONV):
        conv = conv + cw_ref[kk:kk + 1, :] * xpad_ref[halo - 3 + kk:halo - 3 + kk + q, :]
    xpad_ref[0:halo, :] = xpad_ref[q:q + halo, :]
    xbc = _silu(conv)
    xs = xbc[:, :SSD_INNER]
    bm = xbc[:, SSD_INNER:SSD_INNER + SSD_BC]
    cm = xbc[:, SSD_INNER + SSD_BC:]

    dt_in = dt_ref[...] + dtb_ref[...]
    dtv = jnp.maximum(dt_in, 0.0) + jnp.log1p(jnp.exp(-jnp.abs(dt_in)))
    da = dtv * (-jnp.exp(alog_ref[...]))
    tril = tril_ref[...]
    acs = jnp.dot(tril, da, precision=HIGHEST, preferred_element_type=F32)
    acs_t = acs.T
    acs_last = acs[q - 1:q, :]
    dte = jnp.exp(acs_last - acs)
    eacs = jnp.exp(acs)
    cdec = jnp.broadcast_to(jnp.exp(acs_last), (8, LANES))
    stacked = jnp.concatenate([dtv, dte, eacs, cdec], axis=0)
    expanded = jnp.dot(stacked, e_ref[...], precision=HIGHEST, preferred_element_type=F32)
    dt_x = expanded[0:q]
    dte_x = expanded[q:2 * q]
    eacs_x = expanded[2 * q:3 * q]
    cdec_x = expanded[3 * q:3 * q + 1]

    xdt = xs * dt_x
    xdt_b = xdt.astype(BF16)
    wst_b = (xdt * dte_x).astype(BF16)
    causal = tril > 0.5
    lane = lax.broadcasted_iota(I32, (q, LANES), 1)
    lo_half = lane < SSD_HEAD_DIM
    heads_per_group = SSD_HEADS // SSD_GROUPS
    zs = (z0_ref, z1_ref)

    for g in range(SSD_GROUPS):
        gs = slice(g * GROUP_W, (g + 1) * GROUP_W)
        b_g = bm[:, g * SSD_STATE:(g + 1) * SSD_STATE]
        c_gb = cm[:, g * SSD_STATE:(g + 1) * SSD_STATE].astype(BF16)
        b_gb = b_g.astype(BF16)
        cb = lax.dot_general(c_gb, b_gb, (((1,), (1,)), ((), ())), preferred_element_type=F32)
        prev = st_ref[g]
        y_off = jnp.dot(c_gb, prev.astype(BF16), preferred_element_type=F32) * eacs_x[:, gs]
        st_ref[g] = prev * cdec_x[:, gs] + jnp.dot(b_g.T.astype(BF16), wst_b[:, gs],
                                                   preferred_element_type=F32)
        pairs = []
        for j in range(heads_per_group // 2):
            ms = []
            for h in (g * heads_per_group + 2 * j, g * heads_per_group + 2 * j + 1):
                seg = acs[:, h:h + 1] - acs_t[h:h + 1, :]
                lmat = jnp.where(causal, jnp.exp(seg), 0.0)
                ms.append((cb * lmat).astype(BF16))
            c0 = g * GROUP_W + 2 * j * SSD_HEAD_DIM
            xp = xdt_b[:, c0:c0 + LANES]
            zero = jnp.zeros_like(xp)
            rhs = jnp.concatenate([jnp.where(lo_half, xp, zero), jnp.where(lo_half, zero, xp)], axis=0)
            pairs.append(jnp.dot(jnp.concatenate(ms, axis=1), rhs, preferred_element_type=F32))
        y = jnp.concatenate(pairs, axis=1) + y_off + dsk_ref[:, gs] * xs[:, gs]
        gy = y * _silu(zs[g][...])
        ms2 = jnp.mean(gy * gy, axis=-1, keepdims=True)
        y_ref[:, gs] = (gy * lax.rsqrt(ms2 + RMS_EPS) * g_ref[:, gs]).astype(y_ref.dtype)


def _ssd(pf, bsz, seqlen, conv_w, conv_b, dt_bias, a_log, d_skip, norm_g):
    nc = seqlen // SSD_CHUNK
    q = SSD_CHUNK
    pad = LANES - SSD_HEADS
    dtb = jnp.pad(dt_bias.astype(F32), (0, pad)).reshape(1, LANES)
    alog = jnp.pad(a_log.astype(F32), (0, pad)).reshape(1, LANES)
    dsk = jnp.repeat(d_skip.astype(F32), SSD_HEAD_DIM).reshape(1, SSD_INNER)
    expand = (jnp.arange(LANES)[:, None] == (jnp.arange(SSD_INNER)[None, :] // SSD_HEAD_DIM)).astype(F32)
    tril = jnp.tril(jnp.ones((q, q), F32))
    row = lambda b, c: b * nc + c
    const = lambda b, c: (0, 0)
    return pl.pallas_call(
        _ssd_kernel,
        out_shape=jax.ShapeDtypeStruct((bsz * seqlen, SSD_INNER), BF16),
        grid=(bsz, nc),
        in_specs=[
            pl.BlockSpec((q, SSD_CONV_DIM), lambda b, c: (row(b, c), 0)),
            pl.BlockSpec((q, GROUP_W), lambda b, c: (row(b, c), PF_Z_BLK)),
            pl.BlockSpec((q, GROUP_W), lambda b, c: (row(b, c), PF_Z_BLK + 1)),
            pl.BlockSpec((q, LANES), lambda b, c: (row(b, c), PF_DT_BLK)),
            pl.BlockSpec((SSD_CONV, SSD_CONV_DIM), const),
            pl.BlockSpec((1, SSD_CONV_DIM), const),
            pl.BlockSpec((1, LANES), const),
            pl.BlockSpec((1, LANES), const),
            pl.BlockSpec((1, SSD_INNER), const),
            pl.BlockSpec((1, SSD_INNER), const),
            pl.BlockSpec((LANES, SSD_INNER), const),
            pl.BlockSpec((q, q), const),
        ],
        out_specs=pl.BlockSpec((q, SSD_INNER), lambda b, c: (row(b, c), 0)),
        scratch_shapes=[pltpu.VMEM((q + 8, SSD_CONV_DIM), F32),
                        pltpu.VMEM((SSD_GROUPS, SSD_STATE, GROUP_W), F32)],
        compiler_params=pltpu.CompilerParams(
            dimension_semantics=("parallel", "arbitrary"), vmem_limit_bytes=VMEM_LIMIT),
        name="ssd",
    )(pf, pf, pf, pf, conv_w.astype(F32), conv_b.astype(F32).reshape(1, -1), dtb, alog, dsk,
      norm_g.astype(F32).reshape(1, -1), expand, tril)


def _dsa_kernel(q_ref, k_ref, v_ref, qi_ref, idx_all_ref, idx_q_ref, ng_ref, nb_ref, o_ref,
                ki_s, k4_s, vt_s, key_s, qt_s, m_s, acc_s, *, seqlen, topk):
    tk = min(KEY_TILE, seqlen)
    qb = pl.program_id(1)
    n_kt = (qb * Q_BLOCK + Q_BLOCK + tk - 1) // tk
    qpos = qb * Q_BLOCK + lax.broadcasted_iota(I32, (1, Q_BLOCK), 1)
    vrows = ATT_HEAD_DIM + V_ONES_ROWS

    def key_rows(kt):
        return pl.ds(pl.multiple_of(kt * tk, tk), tk)

    def key_pos(kt):
        return kt * tk + lax.broadcasted_iota(I32, (tk, 1), 0)

    @pl.when(qb == 0)
    def _prep():
        vt_s[...] = jnp.ones(vt_s.shape, BF16)

        def body(i, carry):
            rows = key_rows(i)
            kin = idx_all_ref[rows, :][:, :IDX_DIM]
            ki_s[rows, :] = _layer_norm_rows(kin, ng_ref[...], nb_ref[...]).astype(BF16)
            kt_all = k_ref[rows, :]
            vt = v_ref[rows, :].astype(F32).T
            for n in range(ATT_KV_HEADS):
                k4_s[n, rows, :] = kt_all[:, n * ATT_HEAD_DIM:(n + 1) * ATT_HEAD_DIM]
                vt_s[n * vrows:n * vrows + ATT_HEAD_DIM, rows] = (
                    vt[n * ATT_HEAD_DIM:(n + 1) * ATT_HEAD_DIM].astype(BF16))
            return carry
        lax.fori_loop(0, seqlen // tk, body, 0)

    w_t = idx_q_ref[...].T[IDX_DIM:IDX_DIM + IDX_HEADS, :] * (IDX_HEADS ** -0.5) * (IDX_DIM ** -0.5)
    qi_t = qi_ref[...].astype(F32).T.astype(BF16)
    qi_rhs = jnp.concatenate([qi_t[h * IDX_DIM:(h + 1) * IDX_DIM] for h in range(IDX_HEADS)], axis=1)

    def score_tile(kt, carry):
        s_all = jnp.dot(ki_s[key_rows(kt), :], qi_rhs, preferred_element_type=F32)
        acc = jnp.zeros((tk, Q_BLOCK), F32)
        for h in range(IDX_HEADS):
            acc = acc + jnp.maximum(s_all[:, h * Q_BLOCK:(h + 1) * Q_BLOCK], 0.0) * w_t[h:h + 1, :]
        bits = pltpu.bitcast(acc, I32)
        keys = bits ^ ((bits >> 31) & jnp.int32(0x7FFFFFFF))
        keys = jnp.where(acc == 0.0, 0, keys)
        key_s[key_rows(kt), :] = jnp.where(key_pos(kt) <= qpos, keys, INT_MIN)
        return carry

    lax.fori_loop(0, n_kt, score_tile, 0)

    def count_keys(pred):
        def body(kt, cnt):
            m = pred(key_s[key_rows(kt), :], kt).astype(I32)
            return cnt + jnp.sum(m.reshape(tk // 8, 8, Q_BLOCK), axis=0)
        cnt = lax.fori_loop(0, n_kt, body, jnp.zeros((8, Q_BLOCK), I32))
        return jnp.sum(cnt, axis=0, keepdims=True)

    def bit_step(i, t_u):
        cand_u = t_u | jnp.left_shift(jnp.int32(1), 31 - i)
        cand_s = cand_u ^ INT_MIN
        cnt = count_keys(lambda keys, kt: keys >= cand_s)
        return jnp.where(cnt >= topk, cand_u, t_u)

    thr = lax.fori_loop(0, 32, bit_step, jnp.zeros((1, Q_BLOCK), I32)) ^ INT_MIN
    n_gt = count_keys(lambda keys, kt: keys > thr)
    n_eq = count_keys(lambda keys, kt: keys == thr)
    need = topk - n_gt

    def tie_cut():
        nbits = (seqlen - 1).bit_length()

        def idx_step(i, p):
            cand = p | jnp.left_shift(jnp.int32(1), nbits - 1 - i)
            cnt = count_keys(lambda keys, kt: (keys == thr) & (key_pos(kt) < cand))
            return jnp.where(cnt < need, cand, p)
        return lax.fori_loop(0, nbits, idx_step, jnp.zeros((1, Q_BLOCK), I32))

    excess = jnp.max(jnp.where((n_eq > need) & (thr != INT_MIN), 1, 0))
    cut = lax.cond(excess > 0, tie_cut, lambda: jnp.full((1, Q_BLOCK), seqlen, I32))

    q_t = (q_ref[...] * jnp.asarray(ATT_HEAD_DIM ** -0.5, BF16)).astype(F32).T.astype(BF16)
    for n in range(ATT_KV_HEADS):
        qt_s[n] = jnp.concatenate(
            [q_t[(n * N_REP + g) * ATT_HEAD_DIM:(n * N_REP + g + 1) * ATT_HEAD_DIM] for g in range(N_REP)],
            axis=1)
    m_s[...] = jnp.full(m_s.shape, MASK_VALUE, F32)
    acc_s[...] = jnp.zeros(acc_s.shape, F32)

    def attn_tile(kt, carry):
        rows = key_rows(kt)
        keys = key_s[rows, :]
        kpos = key_pos(kt)
        sel = ((keys > thr) | ((keys == thr) & (kpos <= cut))) & (kpos <= qpos)
        bias = jnp.where(sel, 0.0, MASK_VALUE)
        bias4 = jnp.concatenate([bias] * N_REP, axis=1)
        for n in range(ATT_KV_HEADS):
            s = jnp.dot(k4_s[n, rows, :], qt_s[n], preferred_element_type=F32) + bias4
            m_prev = m_s[n]
            m_new = jnp.maximum(m_prev, jnp.max(s, axis=0, keepdims=True))
            alpha = jnp.exp(m_prev - m_new)
            p = jnp.exp(s - m_new).astype(BF16)
            upd = jnp.dot(vt_s[n * vrows:(n + 1) * vrows, rows], p, preferred_element_type=F32)
            acc_s[n] = alpha * acc_s[n] + upd
            m_s[n] = m_new
        return carry

    lax.fori_loop(0, n_kt, attn_tile, 0)

    heads = []
    for n in range(ATT_KV_HEADS):
        a = acc_s[n]
        o_n = a[:ATT_HEAD_DIM] / a[ATT_HEAD_DIM:ATT_HEAD_DIM + 1]
        heads += [o_n[:, g * Q_BLOCK:(g + 1) * Q_BLOCK] for g in range(N_REP)]
    o_ref[...] = jnp.concatenate(heads, axis=0).T.astype(o_ref.dtype)


def _dsa(pb, pf, bsz, seqlen, norm_g, norm_b):
    nb = seqlen // Q_BLOCK
    topk = min(TOPK_MAX, seqlen // 4)
    row = lambda b, i: b * nb + i
    const = lambda b, i: (0, 0)
    lanes4 = N_REP * Q_BLOCK
    vrows = ATT_HEAD_DIM + V_ONES_ROWS
    return pl.pallas_call(
        functools.partial(_dsa_kernel, seqlen=seqlen, topk=topk),
        out_shape=jax.ShapeDtypeStruct((bsz * seqlen, ATT_INNER), BF16),
        grid=(bsz, nb),
        in_specs=[
            pl.BlockSpec((Q_BLOCK, ATT_INNER), lambda b, i: (row(b, i), 0)),
            pl.BlockSpec((seqlen, ATT_KV_W), lambda b, i: (b, PB_K_BLK)),
            pl.BlockSpec((seqlen, ATT_KV_W), lambda b, i: (b, PB_V_BLK)),
            pl.BlockSpec((Q_BLOCK, IDX_HEADS * IDX_DIM), lambda b, i: (row(b, i), PB_QI_BLK)),
            pl.BlockSpec((seqlen, LANES), lambda b, i: (b, PF_IDX_BLK)),
            pl.BlockSpec((Q_BLOCK, LANES), lambda b, i: (row(b, i), PF_IDX_BLK)),
            pl.BlockSpec((1, IDX_DIM), const),
            pl.BlockSpec((1, IDX_DIM), const),
        ],
        out_specs=pl.BlockSpec((Q_BLOCK, ATT_INNER), lambda b, i: (row(b, i), 0)),
        scratch_shapes=[
            pltpu.VMEM((seqlen, IDX_DIM), BF16),
            pltpu.VMEM((ATT_KV_HEADS, seqlen, ATT_HEAD_DIM), BF16),
            pltpu.VMEM((ATT_KV_HEADS * vrows, seqlen), BF16),
            pltpu.VMEM((seqlen, Q_BLOCK), I32),
            pltpu.VMEM((ATT_KV_HEADS, ATT_HEAD_DIM, lanes4), BF16),
            pltpu.VMEM((ATT_KV_HEADS, 1, lanes4), F32),
            pltpu.VMEM((ATT_KV_HEADS, vrows, lanes4), F32),
        ],
        compiler_params=pltpu.CompilerParams(
            dimension_semantics=("parallel", "arbitrary"), vmem_limit_bytes=VMEM_LIMIT),
        name="dsa",
    )(pb, pb, pb, pb, pf, pf, norm_g.astype(F32).reshape(1, -1), norm_b.astype(F32).reshape(1, -1))


def _outproj_kernel(ys_ref, ya_ref, x_ref, w1_ref, w2_ref, g_ref, b_ref, h_ref):
    m = jnp.dot(ys_ref[...], w1_ref[...], preferred_element_type=F32)
    m = m + jnp.dot(ya_ref[...], w2_ref[...], preferred_element_type=F32)
    h_ref[...] = _layer_norm_rows(DEEPNORM_ALPHA * x_ref[...] + m, g_ref[...], b_ref[...])


def _out_proj(y_ssd, y_att, x2d, w_out, ln_g, ln_b, tm):
    n, d = x2d.shape
    w1 = w_out[:SSD_INNER].astype(BF16)
    w2 = w_out[SSD_INNER:].astype(BF16)
    rows = lambda i: (i, 0)
    const = lambda i: (0, 0)
    return pl.pallas_call(
        _outproj_kernel,
        out_shape=jax.ShapeDtypeStruct((n, d), F32),
        grid=(n // tm,),
        in_specs=[pl.BlockSpec((tm, SSD_INNER), rows), pl.BlockSpec((tm, ATT_INNER), rows),
                  pl.BlockSpec((tm, d), rows),
                  pl.BlockSpec((SSD_INNER, d), const), pl.BlockSpec((ATT_INNER, d), const),
                  pl.BlockSpec((1, d), const), pl.BlockSpec((1, d), const)],
        out_specs=pl.BlockSpec((tm, d), rows),
        compiler_params=pltpu.CompilerParams(
            dimension_semantics=("parallel",), vmem_limit_bytes=VMEM_LIMIT),
        name="out_proj_ln",
    )(y_ssd, y_att, x2d, w1, w2, ln_g.astype(F32).reshape(1, -1), ln_b.astype(F32).reshape(1, -1))


FFN_HALO = 16


def _ffn_kernel(h_ref, hprev_ref, wg_ref, wu_ref, cwg_ref, cwu_ref, cbg_ref, cbu_ref, wd_ref,
                g_ref, b_ref, o_ref, hb_ref, acc_ref, *, tiles_per_seq):
    i = pl.program_id(0)
    j = pl.program_id(1)
    tm = h_ref.shape[0]

    @pl.when(j == 0)
    def _stage():
        starts_sequence = (i % tiles_per_seq) == 0
        hb_ref[0:FFN_HALO, :] = jnp.where(starts_sequence, 0.0, hprev_ref[...]).astype(BF16)
        hb_ref[FFN_HALO:, :] = h_ref[...].astype(BF16)
        acc_ref[...] = jnp.zeros_like(acc_ref)

    hb = hb_ref[...]

    def conv_branch(w_ref, cw_ref, cb_ref):
        u = jnp.dot(hb, w_ref[...], preferred_element_type=F32)
        c = cb_ref[...] + cw_ref[FFN_CONV - 1:FFN_CONV, :] * u
        for back in range(1, FFN_CONV):
            c = c + cw_ref[FFN_CONV - 1 - back:FFN_CONV - back, :] * pltpu.roll(u, back, 0)
        return c[FFN_HALO:]

    act = _silu(conv_branch(wg_ref, cwg_ref, cbg_ref)) * conv_branch(wu_ref, cwu_ref, cbu_ref)
    acc_ref[...] += jnp.dot(act.astype(BF16), wd_ref[...], preferred_element_type=F32)

    @pl.when(j == pl.num_programs(1) - 1)
    def _finish():
        o_ref[...] = _layer_norm_rows(DEEPNORM_ALPHA * h_ref[...] + acc_ref[...], g_ref[...], b_ref[...])


def _conv_ffn(h, seqlen, w_up, conv_w, conv_b, w_down, ln_g, ln_b, tm, n_split):
    n, d = h.shape
    d_ff = w_down.shape[0]
    tf = d_ff // n_split
    halo_blocks = tm // FFN_HALO
    w_up_b = w_up.astype(BF16)
    conv_b2 = conv_b.astype(F32).reshape(1, -1)
    rows = lambda i, j: (i, 0)
    const = lambda i, j: (0, 0)
    gate = lambda i, j: (0, j)
    up = lambda i, j: (0, n_split + j)
    return pl.pallas_call(
        functools.partial(_ffn_kernel, tiles_per_seq=seqlen // tm),
        out_shape=jax.ShapeDtypeStruct((n, d), F32),
        grid=(n // tm, n_split),
        in_specs=[pl.BlockSpec((tm, d), rows),
                  pl.BlockSpec((FFN_HALO, d), lambda i, j: (jnp.maximum(i * halo_blocks - 1, 0), 0)),
                  pl.BlockSpec((d, tf), gate), pl.BlockSpec((d, tf), up),
                  pl.BlockSpec((FFN_CONV, tf), gate), pl.BlockSpec((FFN_CONV, tf), up),
                  pl.BlockSpec((1, tf), gate), pl.BlockSpec((1, tf), up),
                  pl.BlockSpec((tf, d), lambda i, j: (j, 0)),
                  pl.BlockSpec((1, d), const), pl.BlockSpec((1, d), const)],
        out_specs=pl.BlockSpec((tm, d), rows),
        scratch_shapes=[pltpu.VMEM((tm + FFN_HALO, d), BF16), pltpu.VMEM((tm, d), F32)],
        compiler_params=pltpu.CompilerParams(
            dimension_semantics=("parallel", "arbitrary"), vmem_limit_bytes=VMEM_LIMIT),
        name="conv_ffn_ln",
    )(h, h, w_up_b, w_up_b, conv_w.astype(F32), conv_w.astype(F32), conv_b2, conv_b2,
      w_down.astype(BF16), ln_g.astype(F32).reshape(1, -1), ln_b.astype(F32).reshape(1, -1))


def _split_w_in(w_in):
    o = 0
    parts = {}
    for name, width in (("z", SSD_INNER), ("xbc", SSD_CONV_DIM), ("dt", SSD_HEADS), ("q", ATT_INNER),
                        ("k", ATT_KV_W), ("v", ATT_KV_W), ("qi", IDX_HEADS * IDX_DIM),
                        ("ki", IDX_DIM), ("wi", IDX_HEADS)):
        parts[name] = w_in[:, o:o + width]
        o += width
    d = w_in.shape[0]
    zeros = lambda w: jnp.zeros((d, w), w_in.dtype)
    wf = jnp.concatenate([parts["xbc"], parts["z"], parts["dt"], zeros(LANES - SSD_HEADS),
                          parts["ki"], parts["wi"], zeros(LANES - IDX_DIM - IDX_HEADS)], axis=1)
    wb = jnp.concatenate([parts["q"], parts["k"], parts["v"], parts["qi"]], axis=1)
    return wf.astype(BF16), wb.astype(BF16)


def _row_tile(n, seqlen, want):
    tm = min(want, seqlen)
    assert n % tm == 0 and seqlen % tm == 0
    return tm


def kernel(x, w_in, ssd_conv_w, ssd_conv_b, dt_bias, a_log, d_skip, ssd_norm_g, idx_k_norm_g,
           idx_k_norm_b, w_out, ln1_g, ln1_b, ffn_w_up, ffn_conv_w, ffn_conv_b, ffn_w_down, ln2_g, ln2_b):
    bsz, seqlen, d = x.shape
    n = bsz * seqlen
    assert seqlen % SSD_CHUNK == 0 and seqlen % Q_BLOCK == 0 and seqlen % min(KEY_TILE, seqlen) == 0
    h = x.reshape(n, d)
    for i in range(DEPTH):
        wf, wb = _split_w_in(w_in[i])
        tm = _row_tile(n, seqlen, 1024)
        pf = _project(h, wf, F32, tm, PF_WIDTH // 2)
        pb = _project(h, wb, BF16, tm, PB_WIDTH // 2)
        y_ssd = _ssd(pf, bsz, seqlen, ssd_conv_w[i], ssd_conv_b[i], dt_bias[i], a_log[i], d_skip[i],
                     ssd_norm_g[i])
        y_att = _dsa(pb, pf, bsz, seqlen, idx_k_norm_g[i], idx_k_norm_b[i])
        h1 = _out_proj(y_ssd, y_att, h, w_out[i], ln1_g[i], ln1_b[i], _row_tile(n, seqlen, 512))
        h = _conv_ffn(h1, seqlen, ffn_w_up[i], ffn_conv_w[i], ffn_conv_b[i], ffn_w_down[i],
                      ln2_g[i], ln2_b[i], _row_tile(n, seqlen, 512), 2)
    return h.reshape(bsz, seqlen, d)
```

```python
import functools

import jax
import jax.numpy as jnp
from jax import lax
from jax.experimental import pallas as pl
from jax.experimental.pallas import tpu as pltpu

F32 = jnp.float32
BF16 = jnp.bfloat16
I32 = jnp.int32
I16 = jnp.int16
HIGHEST = lax.Precision.HIGHEST

SSD_HEADS = 16
SSD_HEAD_DIM = 64
SSD_INNER = SSD_HEADS * SSD_HEAD_DIM
SSD_GROUPS = 2
SSD_STATE = 128
SSD_BC = SSD_GROUPS * SSD_STATE
SSD_CONV = 4
SSD_CONV_DIM = SSD_INNER + 2 * SSD_BC
SSD_CHUNK = 128
GROUP_W = SSD_INNER // SSD_GROUPS
ATT_HEADS = 16
ATT_KV_HEADS = 4
ATT_HEAD_DIM = 64
ATT_INNER = ATT_HEADS * ATT_HEAD_DIM
ATT_KV_W = ATT_KV_HEADS * ATT_HEAD_DIM
N_REP = ATT_HEADS // ATT_KV_HEADS
IDX_HEADS = 8
IDX_DIM = 64
TOPK_MAX = 256
Q_BLOCK = 128
FFN_CONV = 3
LN_EPS = 1e-5
RMS_EPS = 1e-5
DEPTH = 1
DEEPNORM_ALPHA = (2 * DEPTH) ** 0.25

LANES = 128
VMEM_LIMIT = 48 * 1024 * 1024
KEY_TILE = 512
MASK_VALUE = -1e30
INT_MIN = -(2 ** 31)
HALF = 2 ** 15
LOG2E = 1.4426950408889634
V_ONES_ROWS = 16

PF_Z_BLK = SSD_CONV_DIM // GROUP_W
PF_DT_BLK = (SSD_CONV_DIM + SSD_INNER) // LANES
PF_IDX_BLK = PF_DT_BLK + 1
PF_WIDTH = SSD_CONV_DIM + SSD_INNER + 2 * LANES
PB_K_BLK = ATT_INNER // ATT_KV_W
PB_V_BLK = PB_K_BLK + 1
PB_QI_BLK = (ATT_INNER + 2 * ATT_KV_W) // (IDX_HEADS * IDX_DIM)
PB_WIDTH = ATT_INNER + 2 * ATT_KV_W + IDX_HEADS * IDX_DIM


def _layer_norm_rows(r, g, b):
    mu = jnp.mean(r, axis=-1, keepdims=True)
    d = r - mu
    var = jnp.mean(d * d, axis=-1, keepdims=True)
    return d * lax.rsqrt(var + LN_EPS) * g + b


def _silu(x):
    return x * jax.nn.sigmoid(x)


def _matmul_kernel(x_ref, w_ref, o_ref, xb_ref):
    @pl.when(pl.program_id(1) == 0)
    def _cast():
        xb_ref[...] = x_ref[...].astype(BF16)

    o_ref[...] = jnp.dot(xb_ref[...], w_ref[...], preferred_element_type=F32).astype(o_ref.dtype)


def _project(x2d, w, out_dtype, tm, tn):
    n, k = x2d.shape
    width = w.shape[1]
    return pl.pallas_call(
        _matmul_kernel,
        out_shape=jax.ShapeDtypeStruct((n, width), out_dtype),
        grid=(n // tm, width // tn),
        in_specs=[pl.BlockSpec((tm, k), lambda i, j: (i, 0)),
                  pl.BlockSpec((k, tn), lambda i, j: (0, j))],
        out_specs=pl.BlockSpec((tm, tn), lambda i, j: (i, j)),
        scratch_shapes=[pltpu.VMEM((tm, k), BF16)],
        compiler_params=pltpu.CompilerParams(
            dimension_semantics=("parallel", "arbitrary"), vmem_limit_bytes=VMEM_LIMIT),
        name="in_proj",
    )(x2d, w)


def _ssd_kernel(xbc_ref, z0_ref, z1_ref, dt_ref, cw_ref, cb_ref, dtb_ref, alog_ref, dsk_ref,
                g_ref, e_ref, tril_ref, y_ref, xpad_ref, st_ref):
    q = SSD_CHUNK
    halo = 8

    @pl.when(pl.program_id(1) == 0)
    def _init():
        xpad_ref[0:halo, :] = jnp.zeros((halo, SSD_CONV_DIM), F32)
        st_ref[...] = jnp.zeros_like(st_ref)

    xpad_ref[halo:halo + q, :] = xbc_ref[...]
    conv = cb_ref[...] + cw_ref[0:1, :] * xpad_ref[halo - 3:halo - 3 + q, :]
    for kk in range(1, SSD_CONV):
        conv = conv + cw_ref[kk:kk + 1, :] * xpad_ref[halo - 3 + kk:halo - 3 + kk + q, :]
    xpad_ref[0:halo, :] = xpad_ref[q:q + halo, :]
    xbc = _silu(conv)
    xs = xbc[:, :SSD_INNER]
    bm = xbc[:, SSD_INNER:SSD_INNER + SSD_BC]
    cm = xbc[:, SSD_INNER + SSD_BC:]

    dt_in = dt_ref[...] + dtb_ref[...]
    dtv = jnp.maximum(dt_in, 0.0) + jnp.log1p(jnp.exp(-jnp.abs(dt_in)))
    da = dtv * (-jnp.exp(alog_ref[...]))
    tril = tril_ref[...]
    acs = jnp.dot(tril, da, precision=HIGHEST, preferred_element_type=F32)
    acs_t = acs.T
    acs_last = acs[q - 1:q, :]
    dte = jnp.exp(acs_last - acs)
    eacs = jnp.exp(acs)
    cdec = jnp.broadcast_to(jnp.exp(acs_last), (8, LANES))
    stacked = jnp.concatenate([dtv, dte, eacs, cdec], axis=0)
    expanded = jnp.dot(stacked, e_ref[...], precision=HIGHEST, preferred_element_type=F32)
    dt_x = expanded[0:q]
    dte_x = expanded[q:2 * q]
    eacs_x = expanded[2 * q:3 * q]
    cdec_x = expanded[3 * q:3 * q + 1]

    xdt = xs * dt_x
    xdt_b = xdt.astype(BF16)
    wst_b = (xdt * dte_x).astype(BF16)
    causal = tril > 0.5
    lane = lax.broadcasted_iota(I32, (q, LANES), 1)
    lo_half = lane < SSD_HEAD_DIM
    heads_per_group = SSD_HEADS // SSD_GROUPS
    zs = (z0_ref, z1_ref)

    for g in range(SSD_GROUPS):
        gs = slice(g * GROUP_W, (g + 1) * GROUP_W)
        b_g = bm[:, g * SSD_STATE:(g + 1) * SSD_STATE]
        c_gb = cm[:, g * SSD_STATE:(g + 1) * SSD_STATE].astype(BF16)
        b_gb = b_g.astype(BF16)
        cb = lax.dot_general(c_gb, b_gb, (((1,), (1,)), ((), ())), preferred_element_type=F32)
        prev = st_ref[g]
        y_off = jnp.dot(c_gb, prev.astype(BF16), preferred_element_type=F32) * eacs_x[:, gs]
        st_ref[g] = prev * cdec_x[:, gs] + jnp.dot(b_g.T.astype(BF16), wst_b[:, gs],
                                                   preferred_element_type=F32)
        pairs = []
        for j in range(heads_per_group // 2):
            ms = []
            for h in (g * heads_per_group + 2 * j, g * heads_per_group + 2 * j + 1):
                seg = acs[:, h:h + 1] - acs_t[h:h + 1, :]
                lmat = jnp.where(causal, jnp.exp(seg), 0.0)
                ms.append((cb * lmat).astype(BF16))
            c0 = g * GROUP_W + 2 * j * SSD_HEAD_DIM
            xp = xdt_b[:, c0:c0 + LANES]
            zero = jnp.zeros_like(xp)
            rhs = jnp.concatenate([jnp.where(lo_half, xp, zero), jnp.where(lo_half, zero, xp)], axis=0)
            pairs.append(jnp.dot(jnp.concatenate(ms, axis=1), rhs, preferred_element_type=F32))
        y = jnp.concatenate(pairs, axis=1) + y_off + dsk_ref[:, gs] * xs[:, gs]
        gy = y * _silu(zs[g][...])
        ms2 = jnp.mean(gy * gy, axis=-1, keepdims=True)
        y_ref[:, gs] = (gy * lax.rsqrt(ms2 + RMS_EPS) * g_ref[:, gs]).astype(y_ref.dtype)


def _ssd(pf, bsz, seqlen, conv_w, conv_b, dt_bias, a_log, d_skip, norm_g):
    nc = seqlen // SSD_CHUNK
    q = SSD_CHUNK
    pad = LANES - SSD_HEADS
    dtb = jnp.pad(dt_bias.astype(F32), (0, pad)).reshape(1, LANES)
    alog = jnp.pad(a_log.astype(F32), (0, pad)).reshape(1, LANES)
    dsk = jnp.repeat(d_skip.astype(F32), SSD_HEAD_DIM).reshape(1, SSD_INNER)
    expand = (jnp.arange(LANES)[:, None] == (jnp.arange(SSD_INNER)[None, :] // SSD_HEAD_DIM)).astype(F32)
    tril = jnp.tril(jnp.ones((q, q), F32))
    row = lambda b, c: b * nc + c
    const = lambda b, c: (0, 0)
    return pl.pallas_call(
        _ssd_kernel,
        out_shape=jax.ShapeDtypeStruct((bsz * seqlen, SSD_INNER), BF16),
        grid=(bsz, nc),
        in_specs=[
            pl.BlockSpec((q, SSD_CONV_DIM), lambda b, c: (row(b, c), 0)),
            pl.BlockSpec((q, GROUP_W), lambda b, c: (row(b, c), PF_Z_BLK)),
            pl.BlockSpec((q, GROUP_W), lambda b, c: (row(b, c), PF_Z_BLK + 1)),
            pl.BlockSpec((q, LANES), lambda b, c: (row(b, c), PF_DT_BLK)),
            pl.BlockSpec((SSD_CONV, SSD_CONV_DIM), const),
            pl.BlockSpec((1, SSD_CONV_DIM), const),
            pl.BlockSpec((1, LANES), const),
            pl.BlockSpec((1, LANES), const),
            pl.BlockSpec((1, SSD_INNER), const),
            pl.BlockSpec((1, SSD_INNER), const),
            pl.BlockSpec((LANES, SSD_INNER), const),
            pl.BlockSpec((q, q), const),
        ],
        out_specs=pl.BlockSpec((q, SSD_INNER), lambda b, c: (row(b, c), 0)),
        scratch_shapes=[pltpu.VMEM((q + 8, SSD_CONV_DIM), F32),
                        pltpu.VMEM((SSD_GROUPS, SSD_STATE, GROUP_W), F32)],
        compiler_params=pltpu.CompilerParams(
            dimension_semantics=("parallel", "arbitrary"), vmem_limit_bytes=VMEM_LIMIT),
        name="ssd",
    )(pf, pf, pf, pf, conv_w.astype(F32), conv_b.astype(F32).reshape(1, -1), dtb, alog, dsk,
      norm_g.astype(F32).reshape(1, -1), expand, tril)


def _dsa_kernel(q_ref, k_ref, v_ref, qi_ref, idx_all_ref, idx_q_ref, ng_ref, nb_ref, o_ref,
                ki_s, k4_s, vt_s, key_s, half_s, qt_s, acc_s, *, seqlen, topk):
    tk = min(KEY_TILE, seqlen)
    qb = pl.program_id(1)
    n_kt = (qb * Q_BLOCK + Q_BLOCK + tk - 1) // tk
    qpos = qb * Q_BLOCK + lax.broadcasted_iota(I32, (1, Q_BLOCK), 1)
    vrows = ATT_HEAD_DIM + V_ONES_ROWS

    def key_rows(kt):
        return pl.ds(pl.multiple_of(kt * tk, tk), tk)

    def key_pos(kt):
        return kt * tk + lax.broadcasted_iota(I32, (tk, 1), 0)

    @pl.when(qb == 0)
    def _prep():
        vt_s[...] = jnp.ones(vt_s.shape, BF16)

        def body(i, carry):
            rows = key_rows(i)
            kin = idx_all_ref[rows, :][:, :IDX_DIM]
            ki_s[rows, :] = _layer_norm_rows(kin, ng_ref[...], nb_ref[...]).astype(BF16)
            kt_all = k_ref[rows, :]
            vt = v_ref[rows, :].astype(F32).T
            for n in range(ATT_KV_HEADS):
                k4_s[n, rows, :] = kt_all[:, n * ATT_HEAD_DIM:(n + 1) * ATT_HEAD_DIM]
                vt_s[n * vrows:n * vrows + ATT_HEAD_DIM, rows] = (
                    vt[n * ATT_HEAD_DIM:(n + 1) * ATT_HEAD_DIM].astype(BF16))
            return carry
        lax.fori_loop(0, seqlen // tk, body, 0)

    w_t = idx_q_ref[...].T[IDX_DIM:IDX_DIM + IDX_HEADS, :] * (IDX_HEADS ** -0.5) * (IDX_DIM ** -0.5)
    qi_t = qi_ref[...].astype(F32).T.astype(BF16)
    qi_rhs = jnp.concatenate([qi_t[h * IDX_DIM:(h + 1) * IDX_DIM] for h in range(IDX_HEADS)], axis=1)

    def score_tile(kt, carry):
        rows = key_rows(kt)
        s_all = jnp.dot(ki_s[rows, :], qi_rhs, preferred_element_type=F32)
        acc = jnp.zeros((tk, Q_BLOCK), F32)
        for h in range(IDX_HEADS):
            acc = acc + jnp.maximum(s_all[:, h * Q_BLOCK:(h + 1) * Q_BLOCK], 0.0) * w_t[h:h + 1, :]
        bits = pltpu.bitcast(acc, I32)
        keys = bits ^ ((bits >> 31) & jnp.int32(0x7FFFFFFF))
        keys = jnp.where(key_pos(kt) <= qpos, keys, INT_MIN)
        key_s[rows, :] = keys
        half_s[rows, :] = (keys >> 16).astype(I16)
        return carry

    lax.fori_loop(0, n_kt, score_tile, 0)

    def count16(cand):
        def body(kt, cnt):
            m = jnp.where(half_s[key_rows(kt), :] >= cand, jnp.int16(1), jnp.int16(0))
            parts = [m[r * 16:(r + 1) * 16] for r in range(tk // 16)]
            while len(parts) > 1:
                parts = [a + b for a, b in zip(parts[0::2], parts[1::2])]
            return cnt + parts[0]
        cnt = lax.fori_loop(0, n_kt, body, jnp.zeros((16, Q_BLOCK), I16))
        return jnp.sum(cnt.astype(I32), axis=0, keepdims=True)

    def select16(base):
        def step(i, t_u):
            cand_u = t_u | jnp.left_shift(jnp.int32(1), 15 - i)
            cnt = base + count16((cand_u - HALF).astype(I16))
            return jnp.where(cnt >= topk, cand_u, t_u)
        return lax.fori_loop(0, 16, step, jnp.zeros((1, Q_BLOCK), I32))

    def count_keys(pred):
        def body(kt, cnt):
            m = pred(key_s[key_rows(kt), :], kt).astype(I32)
            return cnt + jnp.sum(m.reshape(tk // 8, 8, Q_BLOCK), axis=0)
        cnt = lax.fori_loop(0, n_kt, body, jnp.zeros((8, Q_BLOCK), I32))
        return jnp.sum(cnt, axis=0, keepdims=True)

    thr_hi = select16(0) - HALF

    def low_half_tile(kt, cnt):
        rows = key_rows(kt)
        keys = key_s[rows, :]
        hi = keys >> 16
        half_s[rows, :] = jnp.where(hi == thr_hi, (keys & 0xFFFF) - HALF, -HALF).astype(I16)
        return cnt + jnp.sum((hi > thr_hi).astype(I32).reshape(tk // 8, 8, Q_BLOCK), axis=0)

    n_hi_gt = jnp.sum(lax.fori_loop(0, n_kt, low_half_tile, jnp.zeros((8, Q_BLOCK), I32)),
                      axis=0, keepdims=True)
    thr = jnp.left_shift(thr_hi, 16) | select16(n_hi_gt)
    n_gt = count_keys(lambda keys, kt: keys > thr)
    n_eq = count_keys(lambda keys, kt: keys == thr)
    need = topk - n_gt

    @pl.when(jnp.max(jnp.where((n_eq > need) & (thr != INT_MIN), 1, 0)) > 0)
    def _break_ties():
        nbits = (seqlen - 1).bit_length()

        def idx_step(i, p):
            cand = p | jnp.left_shift(jnp.int32(1), nbits - 1 - i)
            cnt = count_keys(lambda keys, kt: (keys == thr) & (key_pos(kt) < cand))
            return jnp.where(cnt < need, cand, p)
        cut = lax.fori_loop(0, nbits, idx_step, jnp.zeros((1, Q_BLOCK), I32))
        below = jnp.maximum(thr, INT_MIN + 1) - 1

        def demote(kt, carry):
            rows = key_rows(kt)
            keys = key_s[rows, :]
            key_s[rows, :] = jnp.where((keys == thr) & (key_pos(kt) > cut), below, keys)
            return carry
        lax.fori_loop(0, n_kt, demote, 0)

    thr_sel = jnp.where(thr == INT_MIN, INT_MIN + 1, thr)

    q_t = q_ref[...].astype(F32).T.astype(BF16)
    for n in range(ATT_KV_HEADS):
        qt_s[n] = jnp.concatenate(
            [q_t[(n * N_REP + g) * ATT_HEAD_DIM:(n * N_REP + g + 1) * ATT_HEAD_DIM] for g in range(N_REP)],
            axis=1)
    lanes4 = N_REP * Q_BLOCK

    def logits(n, rows, bias4):
        return jnp.dot(k4_s[n, rows, :], qt_s[n], preferred_element_type=F32) + bias4

    def max_tile(kt, ms):
        rows = key_rows(kt)
        bias = jnp.where(key_s[rows, :] >= thr_sel, 0.0, MASK_VALUE)
        key_s[rows, :] = pltpu.bitcast(bias, I32)
        bias4 = jnp.concatenate([bias] * N_REP, axis=1)
        return tuple(jnp.maximum(ms[n], jnp.max(logits(n, rows, bias4).reshape(tk // 8, 8, lanes4), axis=0))
                     for n in range(ATT_KV_HEADS))

    ms = lax.fori_loop(0, n_kt, max_tile,
                       tuple(jnp.full((8, lanes4), MASK_VALUE, F32) for _ in range(ATT_KV_HEADS)))
    m_rows = [jnp.max(m, axis=0, keepdims=True) for m in ms]

    acc_s[...] = jnp.zeros(acc_s.shape, F32)

    def pv_tile(kt, carry):
        rows = key_rows(kt)
        bias = pltpu.bitcast(key_s[rows, :], F32)
        bias4 = jnp.concatenate([bias] * N_REP, axis=1)
        for n in range(ATT_KV_HEADS):
            p = jnp.exp2(logits(n, rows, bias4) - m_rows[n]).astype(BF16)
            acc_s[n] += jnp.dot(vt_s[n * vrows:(n + 1) * vrows, rows], p, preferred_element_type=F32)
        return carry

    lax.fori_loop(0, n_kt, pv_tile, 0)

    heads = []
    for n in range(ATT_KV_HEADS):
        a = acc_s[n]
        o_n = a[:ATT_HEAD_DIM] / a[ATT_HEAD_DIM:ATT_HEAD_DIM + 1]
        heads += [o_n[:, g * Q_BLOCK:(g + 1) * Q_BLOCK] for g in range(N_REP)]
    o_ref[...] = jnp.concatenate(heads, axis=0).T.astype(o_ref.dtype)


def _dsa(pb, pf, bsz, seqlen, norm_g, norm_b):
    nb = seqlen // Q_BLOCK
    topk = min(TOPK_MAX, seqlen // 4)
    row = lambda b, i: b * nb + i
    const = lambda b, i: (0, 0)
    lanes4 = N_REP * Q_BLOCK
    vrows = ATT_HEAD_DIM + V_ONES_ROWS
    return pl.pallas_call(
        functools.partial(_dsa_kernel, seqlen=seqlen, topk=topk),
        out_shape=jax.ShapeDtypeStruct((bsz * seqlen, ATT_INNER), BF16),
        grid=(bsz, nb),
        in_specs=[
            pl.BlockSpec((Q_BLOCK, ATT_INNER), lambda b, i: (row(b, i), 0)),
            pl.BlockSpec((seqlen, ATT_KV_W), lambda b, i: (b, PB_K_BLK)),
            pl.BlockSpec((seqlen, ATT_KV_W), lambda b, i: (b, PB_V_BLK)),
            pl.BlockSpec((Q_BLOCK, IDX_HEADS * IDX_DIM), lambda b, i: (row(b, i), PB_QI_BLK)),
            pl.BlockSpec((seqlen, LANES), lambda b, i: (b, PF_IDX_BLK)),
            pl.BlockSpec((Q_BLOCK, LANES), lambda b, i: (row(b, i), PF_IDX_BLK)),
            pl.BlockSpec((1, IDX_DIM), const),
            pl.BlockSpec((1, IDX_DIM), const),
        ],
        out_specs=pl.BlockSpec((Q_BLOCK, ATT_INNER), lambda b, i: (row(b, i), 0)),
        scratch_shapes=[
            pltpu.VMEM((seqlen, IDX_DIM), BF16),
            pltpu.VMEM((ATT_KV_HEADS, seqlen, ATT_HEAD_DIM), BF16),
            pltpu.VMEM((ATT_KV_HEADS * vrows, seqlen), BF16),
            pltpu.VMEM((seqlen, Q_BLOCK), I32),
            pltpu.VMEM((seqlen, Q_BLOCK), I16),
            pltpu.VMEM((ATT_KV_HEADS, ATT_HEAD_DIM, lanes4), BF16),
            pltpu.VMEM((ATT_KV_HEADS, vrows, lanes4), F32),
        ],
        compiler_params=pltpu.CompilerParams(
            dimension_semantics=("parallel", "arbitrary"), vmem_limit_bytes=VMEM_LIMIT),
        name="dsa",
    )(pb, pb, pb, pb, pf, pf, norm_g.astype(F32).reshape(1, -1), norm_b.astype(F32).reshape(1, -1))


def _outproj_kernel(ys_ref, ya_ref, x_ref, w1_ref, w2_ref, g_ref, b_ref, h_ref):
    m = jnp.dot(ys_ref[...], w1_ref[...], preferred_element_type=F32)
    m = m + jnp.dot(ya_ref[...], w2_ref[...], preferred_element_type=F32)
    h_ref[...] = _layer_norm_rows(DEEPNORM_ALPHA * x_ref[...] + m, g_ref[...], b_ref[...])


def _out_proj(y_ssd, y_att, x2d, w_out, ln_g, ln_b, tm):
    n, d = x2d.shape
    w1 = w_out[:SSD_INNER].astype(BF16)
    w2 = w_out[SSD_INNER:].astype(BF16)
    rows = lambda i: (i, 0)
    const = lambda i: (0, 0)
    return pl.pallas_call(
        _outproj_kernel,
        out_shape=jax.ShapeDtypeStruct((n, d), F32),
        grid=(n // tm,),
        in_specs=[pl.BlockSpec((tm, SSD_INNER), rows), pl.BlockSpec((tm, ATT_INNER), rows),
                  pl.BlockSpec((tm, d), rows),
                  pl.BlockSpec((SSD_INNER, d), const), pl.BlockSpec((ATT_INNER, d), const),
                  pl.BlockSpec((1, d), const), pl.BlockSpec((1, d), const)],
        out_specs=pl.BlockSpec((tm, d), rows),
        compiler_params=pltpu.CompilerParams(
            dimension_semantics=("parallel",), vmem_limit_bytes=VMEM_LIMIT),
        name="out_proj_ln",
    )(y_ssd, y_att, x2d, w1, w2, ln_g.astype(F32).reshape(1, -1), ln_b.astype(F32).reshape(1, -1))


FFN_HALO = 16


def _ffn_kernel(h_ref, hprev_ref, wg_ref, wu_ref, cwg_ref, cwu_ref, cbg_ref, cbu_ref, wd_ref,
                g_ref, b_ref, o_ref, hb_ref, acc_ref, *, tiles_per_seq):
    i = pl.program_id(0)
    j = pl.program_id(1)
    tm = h_ref.shape[0]

    @pl.when(j == 0)
    def _stage():
        starts_sequence = (i % tiles_per_seq) == 0
        hb_ref[0:FFN_HALO, :] = jnp.where(starts_sequence, 0.0, hprev_ref[...]).astype(BF16)
        hb_ref[FFN_HALO:, :] = h_ref[...].astype(BF16)
        acc_ref[...] = jnp.zeros_like(acc_ref)

    hb = hb_ref[...]

    def conv_branch(w_ref, cw_ref, cb_ref):
        u = jnp.dot(hb, w_ref[...], preferred_element_type=F32)
        c = cb_ref[...] + cw_ref[FFN_CONV - 1:FFN_CONV, :] * u
        for back in range(1, FFN_CONV):
            c = c + cw_ref[FFN_CONV - 1 - back:FFN_CONV - back, :] * pltpu.roll(u, back, 0)
        return c[FFN_HALO:]

    act = _silu(conv_branch(wg_ref, cwg_ref, cbg_ref)) * conv_branch(wu_ref, cwu_ref, cbu_ref)
    acc_ref[...] += jnp.dot(act.astype(BF16), wd_ref[...], preferred_element_type=F32)

    @pl.when(j == pl.num_programs(1) - 1)
    def _finish():
        o_ref[...] = _layer_norm_rows(DEEPNORM_ALPHA * h_ref[...] + acc_ref[...], g_ref[...], b_ref[...])


def _conv_ffn(h, seqlen, w_up, conv_w, conv_b, w_down, ln_g, ln_b, tm, n_split):
    n, d = h.shape
    d_ff = w_down.shape[0]
    tf = d_ff // n_split
    halo_blocks = tm // FFN_HALO
    w_up_b = w_up.astype(BF16)
    conv_b2 = conv_b.astype(F32).reshape(1, -1)
    rows = lambda i, j: (i, 0)
    const = lambda i, j: (0, 0)
    gate = lambda i, j: (0, j)
    up = lambda i, j: (0, n_split + j)
    return pl.pallas_call(
        functools.partial(_ffn_kernel, tiles_per_seq=seqlen // tm),
        out_shape=jax.ShapeDtypeStruct((n, d), F32),
        grid=(n // tm, n_split),
        in_specs=[pl.BlockSpec((tm, d), rows),
                  pl.BlockSpec((FFN_HALO, d), lambda i, j: (jnp.maximum(i * halo_blocks - 1, 0), 0)),
                  pl.BlockSpec((d, tf), gate), pl.BlockSpec((d, tf), up),
                  pl.BlockSpec((FFN_CONV, tf), gate), pl.BlockSpec((FFN_CONV, tf), up),
                  pl.BlockSpec((1, tf), gate), pl.BlockSpec((1, tf), up),
                  pl.BlockSpec((tf, d), lambda i, j: (j, 0)),
                  pl.BlockSpec((1, d), const), pl.BlockSpec((1, d), const)],
        out_specs=pl.BlockSpec((tm, d), rows),
        scratch_shapes=[pltpu.VMEM((tm + FFN_HALO, d), BF16), pltpu.VMEM((tm, d), F32)],
        compiler_params=pltpu.CompilerParams(
            dimension_semantics=("parallel", "arbitrary"), vmem_limit_bytes=VMEM_LIMIT),
        name="conv_ffn_ln",
    )(h, h, w_up_b, w_up_b, conv_w.astype(F32), conv_w.astype(F32), conv_b2, conv_b2,
      w_down.astype(BF16), ln_g.astype(F32).reshape(1, -1), ln_b.astype(F32).reshape(1, -1))


def _split_w_in(w_in):
    o = 0
    parts = {}
    for name, width in (("z", SSD_INNER), ("xbc", SSD_CONV_DIM), ("dt", SSD_HEADS), ("q", ATT_INNER),
                        ("k", ATT_KV_W), ("v", ATT_KV_W), ("qi", IDX_HEADS * IDX_DIM),
                        ("ki", IDX_DIM), ("wi", IDX_HEADS)):
        parts[name] = w_in[:, o:o + width]
        o += width
    d = w_in.shape[0]
    zeros = lambda w: jnp.zeros((d, w), w_in.dtype)
    wf = jnp.concatenate([parts["xbc"], parts["z"], parts["dt"], zeros(LANES - SSD_HEADS),
                          parts["ki"], parts["wi"], zeros(LANES - IDX_DIM - IDX_HEADS)], axis=1)
    q_scaled = parts["q"] * (ATT_HEAD_DIM ** -0.5 * LOG2E)
    wb = jnp.concatenate([q_scaled, parts["k"], parts["v"], parts["qi"]], axis=1)
    return wf.astype(BF16), wb.astype(BF16)


def _row_tile(n, seqlen, want):
    tm = min(want, seqlen)
    assert n % tm == 0 and seqlen % tm == 0
    return tm


def kernel(x, w_in, ssd_conv_w, ssd_conv_b, dt_bias, a_log, d_skip, ssd_norm_g, idx_k_norm_g,
           idx_k_norm_b, w_out, ln1_g, ln1_b, ffn_w_up, ffn_conv_w, ffn_conv_b, ffn_w_down, ln2_g, ln2_b):
    bsz, seqlen, d = x.shape
    n = bsz * seqlen
    assert seqlen % SSD_CHUNK == 0 and seqlen % Q_BLOCK == 0 and seqlen % min(KEY_TILE, seqlen) == 0
    h = x.reshape(n, d)
    for i in range(DEPTH):
        wf, wb = _split_w_in(w_in[i])
        tm = _row_tile(n, seqlen, 1024)
        pf = _project(h, wf, F32, tm, PF_WIDTH // 2)
        pb = _project(h, wb, BF16, tm, PB_WIDTH // 2)
        y_ssd = _ssd(pf, bsz, seqlen, ssd_conv_w[i], ssd_conv_b[i], dt_bias[i], a_log[i], d_skip[i],
                     ssd_norm_g[i])
        y_att = _dsa(pb, pf, bsz, seqlen, idx_k_norm_g[i], idx_k_norm_b[i])
        h1 = _out_proj(y_ssd, y_att, h, w_out[i], ln1_g[i], ln1_b[i], _row_tile(n, seqlen, 512))
        h = _conv_ffn(h1, seqlen, ffn_w_up[i], ffn_conv_w[i], ffn_conv_b[i], ffn_w_down[i],
                      ln2_g[i], ln2_b[i], _row_tile(n, seqlen, 512), 2)
    return h.reshape(bsz, seqlen, d)
```

```python
import functools

import jax
import jax.numpy as jnp
from jax import lax
from jax.experimental import pallas as pl
from jax.experimental.pallas import tpu as pltpu

F32 = jnp.float32
BF16 = jnp.bfloat16
I32 = jnp.int32
I16 = jnp.int16
HIGHEST = lax.Precision.HIGHEST

SSD_HEADS = 16
SSD_HEAD_DIM = 64
SSD_INNER = SSD_HEADS * SSD_HEAD_DIM
SSD_GROUPS = 2
SSD_STATE = 128
SSD_BC = SSD_GROUPS * SSD_STATE
SSD_CONV = 4
SSD_CONV_DIM = SSD_INNER + 2 * SSD_BC
SSD_CHUNK = 128
GROUP_W = SSD_INNER // SSD_GROUPS
ATT_HEADS = 16
ATT_KV_HEADS = 4
ATT_HEAD_DIM = 64
ATT_INNER = ATT_HEADS * ATT_HEAD_DIM
ATT_KV_W = ATT_KV_HEADS * ATT_HEAD_DIM
N_REP = ATT_HEADS // ATT_KV_HEADS
IDX_HEADS = 8
IDX_DIM = 64
TOPK_MAX = 256
Q_BLOCK = 128
FFN_CONV = 3
LN_EPS = 1e-5
RMS_EPS = 1e-5
DEPTH = 1
DEEPNORM_ALPHA = (2 * DEPTH) ** 0.25

LANES = 128
VMEM_LIMIT = 48 * 1024 * 1024
KEY_TILE = 512
MASK_VALUE = -1e30
INT_MIN = -(2 ** 31)
HALF = 2 ** 15
LOG2E = 1.4426950408889634
V_ONES_ROWS = 16

PF_Z_BLK = SSD_CONV_DIM // GROUP_W
PF_DT_BLK = (SSD_CONV_DIM + SSD_INNER) // LANES
PF_IDX_BLK = PF_DT_BLK + 1
PF_WIDTH = SSD_CONV_DIM + SSD_INNER + 2 * LANES
PB_K_BLK = ATT_INNER // ATT_KV_W
PB_V_BLK = PB_K_BLK + 1
PB_QI_BLK = (ATT_INNER + 2 * ATT_KV_W) // (IDX_HEADS * IDX_DIM)
PB_WIDTH = ATT_INNER + 2 * ATT_KV_W + IDX_HEADS * IDX_DIM


def _layer_norm_rows(r, g, b):
    mu = jnp.mean(r, axis=-1, keepdims=True)
    d = r - mu
    var = jnp.mean(d * d, axis=-1, keepdims=True)
    return d * lax.rsqrt(var + LN_EPS) * g + b


def _silu(x):
    return x * jax.nn.sigmoid(x)


def _matmul_kernel(x_ref, w_ref, o_ref, xb_ref):
    @pl.when(pl.program_id(1) == 0)
    def _cast():
        xb_ref[...] = x_ref[...].astype(BF16)

    o_ref[...] = jnp.dot(xb_ref[...], w_ref[...], preferred_element_type=F32).astype(o_ref.dtype)


def _project(x2d, w, out_dtype, tm, tn):
    n, k = x2d.shape
    width = w.shape[1]
    return pl.pallas_call(
        _matmul_kernel,
        out_shape=jax.ShapeDtypeStruct((n, width), out_dtype),
        grid=(n // tm, width // tn),
        in_specs=[pl.BlockSpec((tm, k), lambda i, j: (i, 0)),
                  pl.BlockSpec((k, tn), lambda i, j: (0, j))],
        out_specs=pl.BlockSpec((tm, tn), lambda i, j: (i, j)),
        scratch_shapes=[pltpu.VMEM((tm, k), BF16)],
        compiler_params=pltpu.CompilerParams(
            dimension_semantics=("parallel", "arbitrary"), vmem_limit_bytes=VMEM_LIMIT),
        name="in_proj",
    )(x2d, w)


def _ssd_kernel(xbc_ref, z0_ref, z1_ref, dt_ref, cw_ref, cb_ref, dtb_ref, alog_ref, dsk_ref,
                g_ref, e_ref, tril_ref, y_ref, xpad_ref, st_ref):
    q = SSD_CHUNK
    halo = 8

    @pl.when(pl.program_id(1) == 0)
    def _init():
        xpad_ref[0:halo, :] = jnp.zeros((halo, SSD_CONV_DIM), F32)
        st_ref[...] = jnp.zeros_like(st_ref)

    xpad_ref[halo:halo + q, :] = xbc_ref[...]
    conv = cb_ref[...] + cw_ref[0:1, :] * xpad_ref[halo - 3:halo - 3 + q, :]
    for kk in range(1, SSD_CONV):
        conv = conv + cw_ref[kk:kk + 1, :] * xpad_ref[halo - 3 + kk:halo - 3 + kk + q, :]
    xpad_ref[0:halo, :] = xpad_ref[q:q + halo, :]
    xbc = _silu(conv)
    xs = xbc[:, :SSD_INNER]
    bm = xbc[:, SSD_INNER:SSD_INNER + SSD_BC]
    cm = xbc[:, SSD_INNER + SSD_BC:]

    dt_in = dt_ref[...] + dtb_ref[...]
    dtv = jnp.maximum(dt_in, 0.0) + jnp.log1p(jnp.exp(-jnp.abs(dt_in)))
    da = dtv * (-jnp.exp(alog_ref[...]))
    tril = tril_ref[...]
    acs = jnp.dot(tril, da, precision=HIGHEST, preferred_element_type=F32)
    acs_t = acs.T
    acs_last = acs[q - 1:q, :]
    dte = jnp.exp(acs_last - acs)
    eacs = jnp.exp(acs)
    cdec = jnp.broadcast_to(jnp.exp(acs_last), (8, LANES))
    stacked = jnp.concatenate([dtv, dte, eacs, cdec], axis=0)
    expanded = jnp.dot(stacked, e_ref[...], precision=HIGHEST, preferred_element_type=F32)
    dt_x = expanded[0:q]
    dte_x = expanded[q:2 * q]
    eacs_x = expanded[2 * q:3 * q]
    cdec_x = expanded[3 * q:3 * q + 1]

    xdt = xs * dt_x
    xdt_b = xdt.astype(BF16)
    wst_b = (xdt * dte_x).astype(BF16)
    causal = tril > 0.5
    lane = lax.broadcasted_iota(I32, (q, LANES), 1)
    lo_half = lane < SSD_HEAD_DIM
    heads_per_group = SSD_HEADS // SSD_GROUPS
    zs = (z0_ref, z1_ref)

    for g in range(SSD_GROUPS):
        gs = slice(g * GROUP_W, (g + 1) * GROUP_W)
        b_g = bm[:, g * SSD_STATE:(g + 1) * SSD_STATE]
        c_gb = cm[:, g * SSD_STATE:(g + 1) * SSD_STATE].astype(BF16)
        b_gb = b_g.astype(BF16)
        cb = lax.dot_general(c_gb, b_gb, (((1,), (1,)), ((), ())), preferred_element_type=F32)
        prev = st_ref[g]
        y_off = jnp.dot(c_gb, prev.astype(BF16), preferred_element_type=F32) * eacs_x[:, gs]
        st_ref[g] = prev * cdec_x[:, gs] + jnp.dot(b_g.T.astype(BF16), wst_b[:, gs],
                                                   preferred_element_type=F32)
        pairs = []
        for j in range(heads_per_group // 2):
            ms = []
            for h in (g * heads_per_group + 2 * j, g * heads_per_group + 2 * j + 1):
                seg = acs[:, h:h + 1] - acs_t[h:h + 1, :]
                lmat = jnp.where(causal, jnp.exp(seg), 0.0)
                ms.append((cb * lmat).astype(BF16))
            c0 = g * GROUP_W + 2 * j * SSD_HEAD_DIM
            xp = xdt_b[:, c0:c0 + LANES]
            zero = jnp.zeros_like(xp)
            rhs = jnp.concatenate([jnp.where(lo_half, xp, zero), jnp.where(lo_half, zero, xp)], axis=0)
            pairs.append(jnp.dot(jnp.concatenate(ms, axis=1), rhs, preferred_element_type=F32))
        y = jnp.concatenate(pairs, axis=1) + y_off + dsk_ref[:, gs] * xs[:, gs]
        gy = y * _silu(zs[g][...])
        ms2 = jnp.mean(gy * gy, axis=-1, keepdims=True)
        y_ref[:, gs] = (gy * lax.rsqrt(ms2 + RMS_EPS) * g_ref[:, gs]).astype(y_ref.dtype)


def _ssd(pf, bsz, seqlen, conv_w, conv_b, dt_bias, a_log, d_skip, norm_g):
    nc = seqlen // SSD_CHUNK
    q = SSD_CHUNK
    pad = LANES - SSD_HEADS
    dtb = jnp.pad(dt_bias.astype(F32), (0, pad)).reshape(1, LANES)
    alog = jnp.pad(a_log.astype(F32), (0, pad)).reshape(1, LANES)
    dsk = jnp.repeat(d_skip.astype(F32), SSD_HEAD_DIM).reshape(1, SSD_INNER)
    expand = (jnp.arange(LANES)[:, None] == (jnp.arange(SSD_INNER)[None, :] // SSD_HEAD_DIM)).astype(F32)
    tril = jnp.tril(jnp.ones((q, q), F32))
    row = lambda b, c: b * nc + c
    const = lambda b, c: (0, 0)
    return pl.pallas_call(
        _ssd_kernel,
        out_shape=jax.ShapeDtypeStruct((bsz * seqlen, SSD_INNER), BF16),
        grid=(bsz, nc),
        in_specs=[
            pl.BlockSpec((q, SSD_CONV_DIM), lambda b, c: (row(b, c), 0)),
            pl.BlockSpec((q, GROUP_W), lambda b, c: (row(b, c), PF_Z_BLK)),
            pl.BlockSpec((q, GROUP_W), lambda b, c: (row(b, c), PF_Z_BLK + 1)),
            pl.BlockSpec((q, LANES), lambda b, c: (row(b, c), PF_DT_BLK)),
            pl.BlockSpec((SSD_CONV, SSD_CONV_DIM), const),
            pl.BlockSpec((1, SSD_CONV_DIM), const),
            pl.BlockSpec((1, LANES), const),
            pl.BlockSpec((1, LANES), const),
            pl.BlockSpec((1, SSD_INNER), const),
            pl.BlockSpec((1, SSD_INNER), const),
            pl.BlockSpec((LANES, SSD_INNER), const),
            pl.BlockSpec((q, q), const),
        ],
        out_specs=pl.BlockSpec((q, SSD_INNER), lambda b, c: (row(b, c), 0)),
        scratch_shapes=[pltpu.VMEM((q + 8, SSD_CONV_DIM), F32),
                        pltpu.VMEM((SSD_GROUPS, SSD_STATE, GROUP_W), F32)],
        compiler_params=pltpu.CompilerParams(
            dimension_semantics=("parallel", "arbitrary"), vmem_limit_bytes=VMEM_LIMIT),
        name="ssd",
    )(pf, pf, pf, pf, conv_w.astype(F32), conv_b.astype(F32).reshape(1, -1), dtb, alog, dsk,
      norm_g.astype(F32).reshape(1, -1), expand, tril)


def _dsa_kernel(q_ref, k_ref, v_ref, qi_ref, idx_all_ref, idx_q_ref, ng_ref, nb_ref, o_ref,
                ki_s, k4_s, vt_s, key_s, half_s, qt_s, acc_s, m_s, s_s, *, seqlen, topk):
    tk = min(KEY_TILE, seqlen)
    qb = pl.program_id(1)
    n_kt = (qb * Q_BLOCK + Q_BLOCK + tk - 1) // tk
    qpos = qb * Q_BLOCK + lax.broadcasted_iota(I32, (1, Q_BLOCK), 1)
    vrows = ATT_HEAD_DIM + V_ONES_ROWS

    def key_rows(kt):
        return pl.ds(pl.multiple_of(kt * tk, tk), tk)

    def key_pos(kt):
        return kt * tk + lax.broadcasted_iota(I32, (tk, 1), 0)

    @pl.when(qb == 0)
    def _prep():
        vt_s[...] = jnp.ones(vt_s.shape, BF16)

        def body(i, carry):
            rows = key_rows(i)
            kin = idx_all_ref[rows, :][:, :IDX_DIM]
            ki_s[rows, :] = _layer_norm_rows(kin, ng_ref[...], nb_ref[...]).astype(BF16)
            kt_all = k_ref[rows, :]
            vt = v_ref[rows, :].astype(F32).T
            for n in range(ATT_KV_HEADS):
                k4_s[n, rows, :] = kt_all[:, n * ATT_HEAD_DIM:(n + 1) * ATT_HEAD_DIM]
                vt_s[n * vrows:n * vrows + ATT_HEAD_DIM, rows] = (
                    vt[n * ATT_HEAD_DIM:(n + 1) * ATT_HEAD_DIM].astype(BF16))
            return carry
        lax.fori_loop(0, seqlen // tk, body, 0)

    w_t = idx_q_ref[...].T[IDX_DIM:IDX_DIM + IDX_HEADS, :] * (IDX_HEADS ** -0.5) * (IDX_DIM ** -0.5)
    qi_t = qi_ref[...].astype(F32).T.astype(BF16)
    qi_rhs = jnp.concatenate([qi_t[h * IDX_DIM:(h + 1) * IDX_DIM] for h in range(IDX_HEADS)], axis=1)

    def score_tile(kt, carry):
        rows = key_rows(kt)
        s_all = jnp.dot(ki_s[rows, :], qi_rhs, preferred_element_type=F32)
        acc = jnp.zeros((tk, Q_BLOCK), F32)
        for h in range(IDX_HEADS):
            acc = acc + jnp.maximum(s_all[:, h * Q_BLOCK:(h + 1) * Q_BLOCK], 0.0) * w_t[h:h + 1, :]
        bits = pltpu.bitcast(acc, I32)
        keys = bits ^ ((bits >> 31) & jnp.int32(0x7FFFFFFF))
        keys = jnp.where(key_pos(kt) <= qpos, keys, INT_MIN)
        key_s[rows, :] = keys
        half_s[rows, :] = (keys >> 16).astype(I16)
        return carry

    lax.fori_loop(0, n_kt, score_tile, 0)

    def count16(cand):
        def body(kt, cnt):
            m = jnp.where(half_s[key_rows(kt), :] >= cand, jnp.int16(1), jnp.int16(0))
            parts = [m[r * 16:(r + 1) * 16] for r in range(tk // 16)]
            while len(parts) > 1:
                parts = [a + b for a, b in zip(parts[0::2], parts[1::2])]
            return cnt + parts[0]
        cnt = lax.fori_loop(0, n_kt, body, jnp.zeros((16, Q_BLOCK), I16))
        return jnp.sum(cnt.astype(I32), axis=0, keepdims=True)

    def select16(base):
        def step(i, t_u):
            cand_u = t_u | jnp.left_shift(jnp.int32(1), 15 - i)
            cnt = base + count16((cand_u - HALF).astype(I16))
            return jnp.where(cnt >= topk, cand_u, t_u)
        return lax.fori_loop(0, 16, step, jnp.zeros((1, Q_BLOCK), I32))

    def count_keys(pred):
        def body(kt, cnt):
            m = pred(key_s[key_rows(kt), :], kt).astype(I32)
            return cnt + jnp.sum(m.reshape(tk // 8, 8, Q_BLOCK), axis=0)
        cnt = lax.fori_loop(0, n_kt, body, jnp.zeros((8, Q_BLOCK), I32))
        return jnp.sum(cnt, axis=0, keepdims=True)

    thr_hi = select16(0) - HALF

    def low_half_tile(kt, cnt):
        rows = key_rows(kt)
        keys = key_s[rows, :]
        hi = keys >> 16
        half_s[rows, :] = jnp.where(hi == thr_hi, (keys & 0xFFFF) - HALF, -HALF).astype(I16)
        return cnt + jnp.sum((hi > thr_hi).astype(I32).reshape(tk // 8, 8, Q_BLOCK), axis=0)

    n_hi_gt = jnp.sum(lax.fori_loop(0, n_kt, low_half_tile, jnp.zeros((8, Q_BLOCK), I32)),
                      axis=0, keepdims=True)
    thr = jnp.left_shift(thr_hi, 16) | select16(n_hi_gt)
    n_gt = count_keys(lambda keys, kt: keys > thr)
    n_eq = count_keys(lambda keys, kt: keys == thr)
    need = topk - n_gt

    @pl.when(jnp.max(jnp.where((n_eq > need) & (thr != INT_MIN), 1, 0)) > 0)
    def _break_ties():
        nbits = (seqlen - 1).bit_length()

        def idx_step(i, p):
            cand = p | jnp.left_shift(jnp.int32(1), nbits - 1 - i)
            cnt = count_keys(lambda keys, kt: (keys == thr) & (key_pos(kt) < cand))
            return jnp.where(cnt < need, cand, p)
        cut = lax.fori_loop(0, nbits, idx_step, jnp.zeros((1, Q_BLOCK), I32))
        below = jnp.maximum(thr, INT_MIN + 1) - 1

        def demote(kt, carry):
            rows = key_rows(kt)
            keys = key_s[rows, :]
            key_s[rows, :] = jnp.where((keys == thr) & (key_pos(kt) > cut), below, keys)
            return carry
        lax.fori_loop(0, n_kt, demote, 0)

    thr_sel = jnp.where(thr == INT_MIN, INT_MIN + 1, thr)

    q_t = q_ref[...].astype(F32).T.astype(BF16)
    for n in range(ATT_KV_HEADS):
        qt_s[n] = jnp.concatenate(
            [q_t[(n * N_REP + g) * ATT_HEAD_DIM:(n * N_REP + g + 1) * ATT_HEAD_DIM] for g in range(N_REP)],
            axis=1)
    lanes4 = N_REP * Q_BLOCK

    def logits(n, rows, bias4):
        return jnp.dot(k4_s[n, rows, :], qt_s[n], preferred_element_type=F32) + bias4

    acc_s[...] = jnp.zeros(acc_s.shape, F32)
    m_s[...] = jnp.full(m_s.shape, MASK_VALUE, F32)

    def attn_tile(kt, carry):
        rows = key_rows(kt)
        bias = jnp.where(key_s[rows, :] >= thr_sel, 0.0, MASK_VALUE)
        bias4 = jnp.concatenate([bias] * N_REP, axis=1)
        m_new = []
        for n in range(ATT_KV_HEADS):
            s = logits(n, rows, bias4)
            s_s[n] = s
            tile_max = jnp.max(jnp.max(s.reshape(tk // 8, 8, lanes4), axis=0), axis=0, keepdims=True)
            m_new.append(jnp.maximum(m_s[n], tile_max))
        for n in range(ATT_KV_HEADS):
            alpha = jnp.exp2(m_s[n] - m_new[n])
            p = jnp.exp2(s_s[n] - m_new[n]).astype(BF16)
            acc_s[n] = alpha * acc_s[n] + jnp.dot(vt_s[n * vrows:(n + 1) * vrows, rows], p,
                                                  preferred_element_type=F32)
            m_s[n] = m_new[n]
        return carry

    lax.fori_loop(0, n_kt, attn_tile, 0)

    heads = []
    for n in range(ATT_KV_HEADS):
        a = acc_s[n]
        o_n = a[:ATT_HEAD_DIM] / a[ATT_HEAD_DIM:ATT_HEAD_DIM + 1]
        heads += [o_n[:, g * Q_BLOCK:(g + 1) * Q_BLOCK] for g in range(N_REP)]
    o_ref[...] = jnp.concatenate(heads, axis=0).T.astype(o_ref.dtype)


def _dsa(pb, pf, bsz, seqlen, norm_g, norm_b):
    nb = seqlen // Q_BLOCK
    topk = min(TOPK_MAX, seqlen // 4)
    row = lambda b, i: b * nb + i
    const = lambda b, i: (0, 0)
    lanes4 = N_REP * Q_BLOCK
    vrows = ATT_HEAD_DIM + V_ONES_ROWS
    return pl.pallas_call(
        functools.partial(_dsa_kernel, seqlen=seqlen, topk=topk),
        out_shape=jax.ShapeDtypeStruct((bsz * seqlen, ATT_INNER), BF16),
        grid=(bsz, nb),
        in_specs=[
            pl.BlockSpec((Q_BLOCK, ATT_INNER), lambda b, i: (row(b, i), 0)),
            pl.BlockSpec((seqlen, ATT_KV_W), lambda b, i: (b, PB_K_BLK)),
            pl.BlockSpec((seqlen, ATT_KV_W), lambda b, i: (b, PB_V_BLK)),
            pl.BlockSpec((Q_BLOCK, IDX_HEADS * IDX_DIM), lambda b, i: (row(b, i), PB_QI_BLK)),
            pl.BlockSpec((seqlen, LANES), lambda b, i: (b, PF_IDX_BLK)),
            pl.BlockSpec((Q_BLOCK, LANES), lambda b, i: (row(b, i), PF_IDX_BLK)),
            pl.BlockSpec((1, IDX_DIM), const),
            pl.BlockSpec((1, IDX_DIM), const),
        ],
        out_specs=pl.BlockSpec((Q_BLOCK, ATT_INNER), lambda b, i: (row(b, i), 0)),
        scratch_shapes=[
            pltpu.VMEM((seqlen, IDX_DIM), BF16),
            pltpu.VMEM((ATT_KV_HEADS, seqlen, ATT_HEAD_DIM), BF16),
            pltpu.VMEM((ATT_KV_HEADS * vrows, seqlen), BF16),
            pltpu.VMEM((seqlen, Q_BLOCK), I32),
            pltpu.VMEM((seqlen, Q_BLOCK), I16),
            pltpu.VMEM((ATT_KV_HEADS, ATT_HEAD_DIM, lanes4), BF16),
            pltpu.VMEM((ATT_KV_HEADS, vrows, lanes4), F32),
            pltpu.VMEM((ATT_KV_HEADS, 1, lanes4), F32),
            pltpu.VMEM((ATT_KV_HEADS, min(KEY_TILE, seqlen), lanes4), F32),
        ],
        compiler_params=pltpu.CompilerParams(
            dimension_semantics=("parallel", "arbitrary"), vmem_limit_bytes=VMEM_LIMIT),
        name="dsa",
    )(pb, pb, pb, pb, pf, pf, norm_g.astype(F32).reshape(1, -1), norm_b.astype(F32).reshape(1, -1))


def _outproj_kernel(ys_ref, ya_ref, x_ref, w1_ref, w2_ref, g_ref, b_ref, h_ref):
    m = jnp.dot(ys_ref[...], w1_ref[...], preferred_element_type=F32)
    m = m + jnp.dot(ya_ref[...], w2_ref[...], preferred_element_type=F32)
    h_ref[...] = _layer_norm_rows(DEEPNORM_ALPHA * x_ref[...] + m, g_ref[...], b_ref[...])


def _out_proj(y_ssd, y_att, x2d, w_out, ln_g, ln_b, tm):
    n, d = x2d.shape
    w1 = w_out[:SSD_INNER].astype(BF16)
    w2 = w_out[SSD_INNER:].astype(BF16)
    rows = lambda i: (i, 0)
    const = lambda i: (0, 0)
    return pl.pallas_call(
        _outproj_kernel,
        out_shape=jax.ShapeDtypeStruct((n, d), F32),
        grid=(n // tm,),
        in_specs=[pl.BlockSpec((tm, SSD_INNER), rows), pl.BlockSpec((tm, ATT_INNER), rows),
                  pl.BlockSpec((tm, d), rows),
                  pl.BlockSpec((SSD_INNER, d), const), pl.BlockSpec((ATT_INNER, d), const),
                  pl.BlockSpec((1, d), const), pl.BlockSpec((1, d), const)],
        out_specs=pl.BlockSpec((tm, d), rows),
        compiler_params=pltpu.CompilerParams(
            dimension_semantics=("parallel",), vmem_limit_bytes=VMEM_LIMIT),
        name="out_proj_ln",
    )(y_ssd, y_att, x2d, w1, w2, ln_g.astype(F32).reshape(1, -1), ln_b.astype(F32).reshape(1, -1))


FFN_HALO = 16


def _ffn_kernel(h_ref, hprev_ref, wg_ref, wu_ref, cwg_ref, cwu_ref, cbg_ref, cbu_ref, wd_ref,
                g_ref, b_ref, o_ref, hb_ref, acc_ref, *, tiles_per_seq):
    i = pl.program_id(0)
    j = pl.program_id(1)
    tm = h_ref.shape[0]

    @pl.when(j == 0)
    def _stage():
        starts_sequence = (i % tiles_per_seq) == 0
        hb_ref[0:FFN_HALO, :] = jnp.where(starts_sequence, 0.0, hprev_ref[...]).astype(BF16)
        hb_ref[FFN_HALO:, :] = h_ref[...].astype(BF16)
        acc_ref[...] = jnp.zeros_like(acc_ref)

    hb = hb_ref[...]

    def conv_branch(w_ref, cw_ref, cb_ref):
        u = jnp.dot(hb, w_ref[...], preferred_element_type=F32)
        c = cb_ref[...] + cw_ref[FFN_CONV - 1:FFN_CONV, :] * u
        for back in range(1, FFN_CONV):
            c = c + cw_ref[FFN_CONV - 1 - back:FFN_CONV - back, :] * pltpu.roll(u, back, 0)
        return c[FFN_HALO:]

    act = _silu(conv_branch(wg_ref, cwg_ref, cbg_ref)) * conv_branch(wu_ref, cwu_ref, cbu_ref)
    acc_ref[...] += jnp.dot(act.astype(BF16), wd_ref[...], preferred_element_type=F32)

    @pl.when(j == pl.num_programs(1) - 1)
    def _finish():
        o_ref[...] = _layer_norm_rows(DEEPNORM_ALPHA * h_ref[...] + acc_ref[...], g_ref[...], b_ref[...])


def _conv_ffn(h, seqlen, w_up, conv_w, conv_b, w_down, ln_g, ln_b, tm, n_split):
    n, d = h.shape
    d_ff = w_down.shape[0]
    tf = d_ff // n_split
    halo_blocks = tm // FFN_HALO
    w_up_b = w_up.astype(BF16)
    conv_b2 = conv_b.astype(F32).reshape(1, -1)
    rows = lambda i, j: (i, 0)
    const = lambda i, j: (0, 0)
    gate = lambda i, j: (0, j)
    up = lambda i, j: (0, n_split + j)
    return pl.pallas_call(
        functools.partial(_ffn_kernel, tiles_per_seq=seqlen // tm),
        out_shape=jax.ShapeDtypeStruct((n, d), F32),
        grid=(n // tm, n_split),
        in_specs=[pl.BlockSpec((tm, d), rows),
                  pl.BlockSpec((FFN_HALO, d), lambda i, j: (jnp.maximum(i * halo_blocks - 1, 0), 0)),
                  pl.BlockSpec((d, tf), gate), pl.BlockSpec((d, tf), up),
                  pl.BlockSpec((FFN_CONV, tf), gate), pl.BlockSpec((FFN_CONV, tf), up),
                  pl.BlockSpec((1, tf), gate), pl.BlockSpec((1, tf), up),
                  pl.BlockSpec((tf, d), lambda i, j: (j, 0)),
                  pl.BlockSpec((1, d), const), pl.BlockSpec((1, d), const)],
        out_specs=pl.BlockSpec((tm, d), rows),
        scratch_shapes=[pltpu.VMEM((tm + FFN_HALO, d), BF16), pltpu.VMEM((tm, d), F32)],
        compiler_params=pltpu.CompilerParams(
            dimension_semantics=("parallel", "arbitrary"), vmem_limit_bytes=VMEM_LIMIT),
        name="conv_ffn_ln",
    )(h, h, w_up_b, w_up_b, conv_w.astype(F32), conv_w.astype(F32), conv_b2, conv_b2,
      w_down.astype(BF16), ln_g.astype(F32).reshape(1, -1), ln_b.astype(F32).reshape(1, -1))


def _split_w_in(w_in):
    o = 0
    parts = {}
    for name, width in (("z", SSD_INNER), ("xbc", SSD_CONV_DIM), ("dt", SSD_HEADS), ("q", ATT_INNER),
                        ("k", ATT_KV_W), ("v", ATT_KV_W), ("qi", IDX_HEADS * IDX_DIM),
                        ("ki", IDX_DIM), ("wi", IDX_HEADS)):
        parts[name] = w_in[:, o:o + width]
        o += width
    d = w_in.shape[0]
    zeros = lambda w: jnp.zeros((d, w), w_in.dtype)
    wf = jnp.concatenate([parts["xbc"], parts["z"], parts["dt"], zeros(LANES - SSD_HEADS),
                          parts["ki"], parts["wi"], zeros(LANES - IDX_DIM - IDX_HEADS)], axis=1)
    q_scaled = parts["q"] * (ATT_HEAD_DIM ** -0.5 * LOG2E)
    wb = jnp.concatenate([q_scaled, parts["k"], parts["v"], parts["qi"]], axis=1)
    return wf.astype(BF16), wb.astype(BF16)


def _row_tile(n, seqlen, want):
    tm = min(want, seqlen)
    assert n % tm == 0 and seqlen % tm == 0
    return tm


def kernel(x, w_in, ssd_conv_w, ssd_conv_b, dt_bias, a_log, d_skip, ssd_norm_g, idx_k_norm_g,
           idx_k_norm_b, w_out, ln1_g, ln1_b, ffn_w_up, ffn_conv_w, ffn_conv_b, ffn_w_down, ln2_g, ln2_b):
    bsz, seqlen, d = x.shape
    n = bsz * seqlen
    assert seqlen % SSD_CHUNK == 0 and seqlen % Q_BLOCK == 0 and seqlen % min(KEY_TILE, seqlen) == 0
    h = x.reshape(n, d)
    for i in range(DEPTH):
        wf, wb = _split_w_in(w_in[i])
        tm = _row_tile(n, seqlen, 1024)
        pf = _project(h, wf, F32, tm, PF_WIDTH // 2)
        pb = _project(h, wb, BF16, tm, PB_WIDTH // 2)
        y_ssd = _ssd(pf, bsz, seqlen, ssd_conv_w[i], ssd_conv_b[i], dt_bias[i], a_log[i], d_skip[i],
                     ssd_norm_g[i])
        y_att = _dsa(pb, pf, bsz, seqlen, idx_k_norm_g[i], idx_k_norm_b[i])
        h1 = _out_proj(y_ssd, y_att, h, w_out[i], ln1_g[i], ln1_b[i], _row_tile(n, seqlen, 512))
        h = _conv_ffn(h1, seqlen, ffn_w_up[i], ffn_conv_w[i], ffn_conv_b[i], ffn_w_down[i],
                      ln2_g[i], ln2_b[i], _row_tile(n, seqlen, 512), 2)
    return h.reshape(bsz, seqlen, d)
```

```python
import functools

import jax
import jax.numpy as jnp
from jax import lax
from jax.experimental import pallas as pl
from jax.experimental.pallas import tpu as pltpu

F32 = jnp.float32
BF16 = jnp.bfloat16
I32 = jnp.int32
HIGHEST = lax.Precision.HIGHEST

SSD_HEADS = 16
SSD_HEAD_DIM = 64
SSD_INNER = SSD_HEADS * SSD_HEAD_DIM
SSD_GROUPS = 2
SSD_STATE = 128
SSD_BC = SSD_GROUPS * SSD_STATE
SSD_CONV = 4
SSD_CONV_DIM = SSD_INNER + 2 * SSD_BC
SSD_CHUNK = 128
GROUP_W = SSD_INNER // SSD_GROUPS
ATT_HEADS = 16
ATT_KV_HEADS = 4
ATT_HEAD_DIM = 64
ATT_INNER = ATT_HEADS * ATT_HEAD_DIM
ATT_KV_W = ATT_KV_HEADS * ATT_HEAD_DIM
N_REP = ATT_HEADS // ATT_KV_HEADS
IDX_HEADS = 8
IDX_DIM = 64
TOPK_MAX = 256
Q_BLOCK = 128
FFN_CONV = 3
LN_EPS = 1e-5
RMS_EPS = 1e-5
DEPTH = 1
DEEPNORM_ALPHA = (2 * DEPTH) ** 0.25

LANES = 128
VMEM_LIMIT = 48 * 1024 * 1024
KEY_TILE = 512
MASK_VALUE = -1e30
INT_MIN = -(2 ** 31)
LOG2E = 1.4426950408889634
V_ONES_ROWS = 16

PF_Z_BLK = SSD_CONV_DIM // GROUP_W
PF_DT_BLK = (SSD_CONV_DIM + SSD_INNER) // LANES
PF_IDX_BLK = PF_DT_BLK + 1
PF_WIDTH = SSD_CONV_DIM + SSD_INNER + 2 * LANES
PB_K_BLK = ATT_INNER // ATT_KV_W
PB_V_BLK = PB_K_BLK + 1
PB_QI_BLK = (ATT_INNER + 2 * ATT_KV_W) // (IDX_HEADS * IDX_DIM)
PB_WIDTH = ATT_INNER + 2 * ATT_KV_W + IDX_HEADS * IDX_DIM


def _layer_norm_rows(r, g, b):
    mu = jnp.mean(r, axis=-1, keepdims=True)
    d = r - mu
    var = jnp.mean(d * d, axis=-1, keepdims=True)
    return d * lax.rsqrt(var + LN_EPS) * g + b


def _silu(x):
    return x * jax.nn.sigmoid(x)


def _matmul_kernel(x_ref, w_ref, o_ref, xb_ref):
    @pl.when(pl.program_id(1) == 0)
    def _cast():
        xb_ref[...] = x_ref[...].astype(BF16)

    o_ref[...] = jnp.dot(xb_ref[...], w_ref[...], preferred_element_type=F32).astype(o_ref.dtype)


def _project(x2d, w, out_dtype, tm, tn):
    n, k = x2d.shape
    width = w.shape[1]
    return pl.pallas_call(
        _matmul_kernel,
        out_shape=jax.ShapeDtypeStruct((n, width), out_dtype),
        grid=(n // tm, width // tn),
        in_specs=[pl.BlockSpec((tm, k), lambda i, j: (i, 0)),
                  pl.BlockSpec((k, tn), lambda i, j: (0, j))],
        out_specs=pl.BlockSpec((tm, tn), lambda i, j: (i, j)),
        scratch_shapes=[pltpu.VMEM((tm, k), BF16)],
        compiler_params=pltpu.CompilerParams(
            dimension_semantics=("parallel", "arbitrary"), vmem_limit_bytes=VMEM_LIMIT),
        name="in_proj",
    )(x2d, w)


def _ssd_kernel(xbc_ref, z0_ref, z1_ref, dt_ref, cw_ref, cb_ref, dtb_ref, alog_ref, dsk_ref,
                g_ref, e_ref, tril_ref, y_ref, xpad_ref, st_ref):
    q = SSD_CHUNK
    halo = 8

    @pl.when(pl.program_id(1) == 0)
    def _init():
        xpad_ref[0:halo, :] = jnp.zeros((halo, SSD_CONV_DIM), F32)
        st_ref[...] = jnp.zeros_like(st_ref)

    xpad_ref[halo:halo + q, :] = xbc_ref[...]
    conv = cb_ref[...] + cw_ref[0:1, :] * xpad_ref[halo - 3:halo - 3 + q, :]
    for kk in range(1, SSD_CONV):
        conv = conv + cw_ref[kk:kk + 1, :] * xpad_ref[halo - 3 + kk:halo - 3 + kk + q, :]
    xpad_ref[0:halo, :] = xpad_ref[q:q + halo, :]
    xbc = _silu(conv)
    xs = xbc[:, :SSD_INNER]
    bm = xbc[:, SSD_INNER:SSD_INNER + SSD_BC]
    cm = xbc[:, SSD_INNER + SSD_BC:]

    dt_in = dt_ref[...] + dtb_ref[...]
    dtv = jnp.maximum(dt_in, 0.0) + jnp.log1p(jnp.exp(-jnp.abs(dt_in)))
    da = dtv * (-jnp.exp(alog_ref[...]))
    tril = tril_ref[...]
    acs = jnp.dot(tril, da, precision=HIGHEST, preferred_element_type=F32)
    acs_t = acs.T
    acs_last = acs[q - 1:q, :]
    dte = jnp.exp(acs_last - acs)
    eacs = jnp.exp(acs)
    cdec = jnp.broadcast_to(jnp.exp(acs_last), (8, LANES))
    stacked = jnp.concatenate([dtv, dte, eacs, cdec], axis=0)
    expanded = jnp.dot(stacked, e_ref[...], precision=HIGHEST, preferred_element_type=F32)
    dt_x = expanded[0:q]
    dte_x = expanded[q:2 * q]
    eacs_x = expanded[2 * q:3 * q]
    cdec_x = expanded[3 * q:3 * q + 1]

    xdt = xs * dt_x
    xdt_b = xdt.astype(BF16)
    wst_b = (xdt * dte_x).astype(BF16)
    causal = tril > 0.5
    lane = lax.broadcasted_iota(I32, (q, LANES), 1)
    lo_half = lane < SSD_HEAD_DIM
    heads_per_group = SSD_HEADS // SSD_GROUPS
    zs = (z0_ref, z1_ref)

    for g in range(SSD_GROUPS):
        gs = slice(g * GROUP_W, (g + 1) * GROUP_W)
        b_g = bm[:, g * SSD_STATE:(g + 1) * SSD_STATE]
        c_gb = cm[:, g * SSD_STATE:(g + 1) * SSD_STATE].astype(BF16)
        b_gb = b_g.astype(BF16)
        cb = lax.dot_general(c_gb, b_gb, (((1,), (1,)), ((), ())), preferred_element_type=F32)
        prev = st_ref[g]
        y_off = jnp.dot(c_gb, prev.astype(BF16), preferred_element_type=F32) * eacs_x[:, gs]
        st_ref[g] = prev * cdec_x[:, gs] + jnp.dot(b_g.T.astype(BF16), wst_b[:, gs],
                                                   preferred_element_type=F32)
        pairs = []
        for j in range(heads_per_group // 2):
            ms = []
            for h in (g * heads_per_group + 2 * j, g * heads_per_group + 2 * j + 1):
                seg = acs[:, h:h + 1] - acs_t[h:h + 1, :]
                lmat = jnp.where(causal, jnp.exp(seg), 0.0)
                ms.append((cb * lmat).astype(BF16))
            c0 = g * GROUP_W + 2 * j * SSD_HEAD_DIM
            xp = xdt_b[:, c0:c0 + LANES]
            zero = jnp.zeros_like(xp)
            rhs = jnp.concatenate([jnp.where(lo_half, xp, zero), jnp.where(lo_half, zero, xp)], axis=0)
            pairs.append(jnp.dot(jnp.concatenate(ms, axis=1), rhs, preferred_element_type=F32))
        y = jnp.concatenate(pairs, axis=1) + y_off + dsk_ref[:, gs] * xs[:, gs]
        gy = y * _silu(zs[g][...])
        ms2 = jnp.mean(gy * gy, axis=-1, keepdims=True)
        y_ref[:, gs] = (gy * lax.rsqrt(ms2 + RMS_EPS) * g_ref[:, gs]).astype(y_ref.dtype)


def _ssd(pf, bsz, seqlen, conv_w, conv_b, dt_bias, a_log, d_skip, norm_g):
    nc = seqlen // SSD_CHUNK
    q = SSD_CHUNK
    pad = LANES - SSD_HEADS
    dtb = jnp.pad(dt_bias.astype(F32), (0, pad)).reshape(1, LANES)
    alog = jnp.pad(a_log.astype(F32), (0, pad)).reshape(1, LANES)
    dsk = jnp.repeat(d_skip.astype(F32), SSD_HEAD_DIM).reshape(1, SSD_INNER)
    expand = (jnp.arange(LANES)[:, None] == (jnp.arange(SSD_INNER)[None, :] // SSD_HEAD_DIM)).astype(F32)
    tril = jnp.tril(jnp.ones((q, q), F32))
    row = lambda b, c: b * nc + c
    const = lambda b, c: (0, 0)
    return pl.pallas_call(
        _ssd_kernel,
        out_shape=jax.ShapeDtypeStruct((bsz * seqlen, SSD_INNER), BF16),
        grid=(bsz, nc),
        in_specs=[
            pl.BlockSpec((q, SSD_CONV_DIM), lambda b, c: (row(b, c), 0)),
            pl.BlockSpec((q, GROUP_W), lambda b, c: (row(b, c), PF_Z_BLK)),
            pl.BlockSpec((q, GROUP_W), lambda b, c: (row(b, c), PF_Z_BLK + 1)),
            pl.BlockSpec((q, LANES), lambda b, c: (row(b, c), PF_DT_BLK)),
            pl.BlockSpec((SSD_CONV, SSD_CONV_DIM), const),
            pl.BlockSpec((1, SSD_CONV_DIM), const),
            pl.BlockSpec((1, LANES), const),
            pl.BlockSpec((1, LANES), const),
            pl.BlockSpec((1, SSD_INNER), const),
            pl.BlockSpec((1, SSD_INNER), const),
            pl.BlockSpec((LANES, SSD_INNER), const),
            pl.BlockSpec((q, q), const),
        ],
        out_specs=pl.BlockSpec((q, SSD_INNER), lambda b, c: (row(b, c), 0)),
        scratch_shapes=[pltpu.VMEM((q + 8, SSD_CONV_DIM), F32),
                        pltpu.VMEM((SSD_GROUPS, SSD_STATE, GROUP_W), F32)],
        compiler_params=pltpu.CompilerParams(
            dimension_semantics=("parallel", "arbitrary"), vmem_limit_bytes=VMEM_LIMIT),
        name="ssd",
    )(pf, pf, pf, pf, conv_w.astype(F32), conv_b.astype(F32).reshape(1, -1), dtb, alog, dsk,
      norm_g.astype(F32).reshape(1, -1), expand, tril)


def _dsa_kernel(q_ref, k_ref, v_ref, qi_ref, idx_all_ref, idx_q_ref, ng_ref, nb_ref, o_ref,
                ki_s, k4_s, vt_s, key_s, qt_s, acc_s, m_s, s_s, *, seqlen, topk):
    tk = min(KEY_TILE, seqlen)
    qb = pl.program_id(1)
    n_kt = (qb * Q_BLOCK + Q_BLOCK + tk - 1) // tk
    qpos = qb * Q_BLOCK + lax.broadcasted_iota(I32, (1, Q_BLOCK), 1)
    vrows = ATT_HEAD_DIM + V_ONES_ROWS

    def key_rows(kt):
        return pl.ds(pl.multiple_of(kt * tk, tk), tk)

    def key_pos(kt):
        return kt * tk + lax.broadcasted_iota(I32, (tk, 1), 0)

    @pl.when(qb == 0)
    def _prep():
        vt_s[...] = jnp.ones(vt_s.shape, BF16)

        def body(i, carry):
            rows = key_rows(i)
            kin = idx_all_ref[rows, :][:, :IDX_DIM]
            ki_s[rows, :] = _layer_norm_rows(kin, ng_ref[...], nb_ref[...]).astype(BF16)
            kt_all = k_ref[rows, :]
            vt = v_ref[rows, :].astype(F32).T
            for n in range(ATT_KV_HEADS):
                k4_s[n, rows, :] = kt_all[:, n * ATT_HEAD_DIM:(n + 1) * ATT_HEAD_DIM]
                vt_s[n * vrows:n * vrows + ATT_HEAD_DIM, rows] = (
                    vt[n * ATT_HEAD_DIM:(n + 1) * ATT_HEAD_DIM].astype(BF16))
            return carry
        lax.fori_loop(0, seqlen // tk, body, 0)

    w_t = idx_q_ref[...].T[IDX_DIM:IDX_DIM + IDX_HEADS, :] * (IDX_HEADS ** -0.5) * (IDX_DIM ** -0.5)
    qi_t = qi_ref[...].astype(F32).T.astype(BF16)
    qi_rhs = jnp.concatenate([qi_t[h * IDX_DIM:(h + 1) * IDX_DIM] for h in range(IDX_HEADS)], axis=1)

    def score_tile(kt, carry):
        rows = key_rows(kt)
        s_all = jnp.dot(ki_s[rows, :], qi_rhs, preferred_element_type=F32)
        acc = jnp.zeros((tk, Q_BLOCK), F32)
        for h in range(IDX_HEADS):
            acc = acc + jnp.maximum(s_all[:, h * Q_BLOCK:(h + 1) * Q_BLOCK], 0.0) * w_t[h:h + 1, :]
        bits = pltpu.bitcast(acc, I32)
        keys = bits ^ ((bits >> 31) & jnp.int32(0x7FFFFFFF))
        keys = jnp.where(key_pos(kt) <= qpos, keys, INT_MIN)
        key_s[rows, :] = keys
        return carry

    lax.fori_loop(0, n_kt, score_tile, 0)

    def count_keys(pred):
        def body(kt, cnt):
            m = pred(key_s[key_rows(kt), :], kt).astype(I32)
            return cnt + jnp.sum(m.reshape(tk // 8, 8, Q_BLOCK), axis=0)
        cnt = lax.fori_loop(0, n_kt, body, jnp.zeros((8, Q_BLOCK), I32))
        return jnp.sum(cnt, axis=0, keepdims=True)

    def bit_step(i, t_u):
        cand_u = t_u | jnp.left_shift(jnp.int32(1), 31 - i)
        cand_s = cand_u ^ INT_MIN
        cnt = count_keys(lambda keys, kt: keys >= cand_s)
        return jnp.where(cnt >= topk, cand_u, t_u)

    thr = lax.fori_loop(0, 32, bit_step, jnp.zeros((1, Q_BLOCK), I32)) ^ INT_MIN
    n_gt = count_keys(lambda keys, kt: keys > thr)
    n_eq = count_keys(lambda keys, kt: keys == thr)
    need = topk - n_gt

    @pl.when(jnp.max(jnp.where((n_eq > need) & (thr != INT_MIN), 1, 0)) > 0)
    def _break_ties():
        nbits = (seqlen - 1).bit_length()

        def idx_step(i, p):
            cand = p | jnp.left_shift(jnp.int32(1), nbits - 1 - i)
            cnt = count_keys(lambda keys, kt: (keys == thr) & (key_pos(kt) < cand))
            return jnp.where(cnt < need, cand, p)
        cut = lax.fori_loop(0, nbits, idx_step, jnp.zeros((1, Q_BLOCK), I32))
        below = jnp.maximum(thr, INT_MIN + 1) - 1

        def demote(kt, carry):
            rows = key_rows(kt)
            keys = key_s[rows, :]
            key_s[rows, :] = jnp.where((keys == thr) & (key_pos(kt) > cut), below, keys)
            return carry
        lax.fori_loop(0, n_kt, demote, 0)

    thr_sel = jnp.where(thr == INT_MIN, INT_MIN + 1, thr)

    q_t = q_ref[...].astype(F32).T.astype(BF16)
    for n in range(ATT_KV_HEADS):
        qt_s[n] = jnp.concatenate(
            [q_t[(n * N_REP + g) * ATT_HEAD_DIM:(n * N_REP + g + 1) * ATT_HEAD_DIM] for g in range(N_REP)],
            axis=1)
    lanes4 = N_REP * Q_BLOCK

    def logits(n, rows, bias4):
        return jnp.dot(k4_s[n, rows, :], qt_s[n], preferred_element_type=F32) + bias4

    acc_s[...] = jnp.zeros(acc_s.shape, F32)
    m_s[...] = jnp.full(m_s.shape, MASK_VALUE, F32)

    def attn_tile(kt, carry):
        rows = key_rows(kt)
        bias = jnp.where(key_s[rows, :] >= thr_sel, 0.0, MASK_VALUE)
        bias4 = jnp.concatenate([bias] * N_REP, axis=1)
        m_new = []
        for n in range(ATT_KV_HEADS):
            s = logits(n, rows, bias4)
            s_s[n] = s
            tile_max = jnp.max(jnp.max(s.reshape(tk // 8, 8, lanes4), axis=0), axis=0, keepdims=True)
            m_new.append(jnp.maximum(m_s[n], tile_max))
        for n in range(ATT_KV_HEADS):
            alpha = jnp.exp2(m_s[n] - m_new[n])
            p = jnp.exp2(s_s[n] - m_new[n]).astype(BF16)
            acc_s[n] = alpha * acc_s[n] + jnp.dot(vt_s[n * vrows:(n + 1) * vrows, rows], p,
                                                  preferred_element_type=F32)
            m_s[n] = m_new[n]
        return carry

    lax.fori_loop(0, n_kt, attn_tile, 0)

    heads = []
    for n in range(ATT_KV_HEADS):
        a = acc_s[n]
        o_n = a[:ATT_HEAD_DIM] / a[ATT_HEAD_DIM:ATT_HEAD_DIM + 1]
        heads += [o_n[:, g * Q_BLOCK:(g + 1) * Q_BLOCK] for g in range(N_REP)]
    o_ref[...] = jnp.concatenate(heads, axis=0).T.astype(o_ref.dtype)


def _dsa(pb, pf, bsz, seqlen, norm_g, norm_b):
    nb = seqlen // Q_BLOCK
    topk = min(TOPK_MAX, seqlen // 4)
    row = lambda b, i: b * nb + i
    const = lambda b, i: (0, 0)
    lanes4 = N_REP * Q_BLOCK
    vrows = ATT_HEAD_DIM + V_ONES_ROWS
    return pl.pallas_call(
        functools.partial(_dsa_kernel, seqlen=seqlen, topk=topk),
        out_shape=jax.ShapeDtypeStruct((bsz * seqlen, ATT_INNER), BF16),
        grid=(bsz, nb),
        in_specs=[
            pl.BlockSpec((Q_BLOCK, ATT_INNER), lambda b, i: (row(b, i), 0)),
            pl.BlockSpec((seqlen, ATT_KV_W), lambda b, i: (b, PB_K_BLK)),
            pl.BlockSpec((seqlen, ATT_KV_W), lambda b, i: (b, PB_V_BLK)),
            pl.BlockSpec((Q_BLOCK, IDX_HEADS * IDX_DIM), lambda b, i: (row(b, i), PB_QI_BLK)),
            pl.BlockSpec((seqlen, LANES), lambda b, i: (b, PF_IDX_BLK)),
            pl.BlockSpec((Q_BLOCK, LANES), lambda b, i: (row(b, i), PF_IDX_BLK)),
            pl.BlockSpec((1, IDX_DIM), const),
            pl.BlockSpec((1, IDX_DIM), const),
        ],
        out_specs=pl.BlockSpec((Q_BLOCK, ATT_INNER), lambda b, i: (row(b, i), 0)),
        scratch_shapes=[
            pltpu.VMEM((seqlen, IDX_DIM), BF16),
            pltpu.VMEM((ATT_KV_HEADS, seqlen, ATT_HEAD_DIM), BF16),
            pltpu.VMEM((ATT_KV_HEADS * vrows, seqlen), BF16),
            pltpu.VMEM((seqlen, Q_BLOCK), I32),
            pltpu.VMEM((ATT_KV_HEADS, ATT_HEAD_DIM, lanes4), BF16),
            pltpu.VMEM((ATT_KV_HEADS, vrows, lanes4), F32),
            pltpu.VMEM((ATT_KV_HEADS, 1, lanes4), F32),
            pltpu.VMEM((ATT_KV_HEADS, min(KEY_TILE, seqlen), lanes4), F32),
        ],
        compiler_params=pltpu.CompilerParams(
            dimension_semantics=("parallel", "arbitrary"), vmem_limit_bytes=VMEM_LIMIT),
        name="dsa",
    )(pb, pb, pb, pb, pf, pf, norm_g.astype(F32).reshape(1, -1), norm_b.astype(F32).reshape(1, -1))


def _outproj_kernel(ys_ref, ya_ref, x_ref, w1_ref, w2_ref, g_ref, b_ref, h_ref):
    m = jnp.dot(ys_ref[...], w1_ref[...], preferred_element_type=F32)
    m = m + jnp.dot(ya_ref[...], w2_ref[...], preferred_element_type=F32)
    h_ref[...] = _layer_norm_rows(DEEPNORM_ALPHA * x_ref[...] + m, g_ref[...], b_ref[...])


def _out_proj(y_ssd, y_att, x2d, w_out, ln_g, ln_b, tm):
    n, d = x2d.shape
    w1 = w_out[:SSD_INNER].astype(BF16)
    w2 = w_out[SSD_INNER:].astype(BF16)
    rows = lambda i: (i, 0)
    const = lambda i: (0, 0)
    return pl.pallas_call(
        _outproj_kernel,
        out_shape=jax.ShapeDtypeStruct((n, d), F32),
        grid=(n // tm,),
        in_specs=[pl.BlockSpec((tm, SSD_INNER), rows), pl.BlockSpec((tm, ATT_INNER), rows),
                  pl.BlockSpec((tm, d), rows),
                  pl.BlockSpec((SSD_INNER, d), const), pl.BlockSpec((ATT_INNER, d), const),
                  pl.BlockSpec((1, d), const), pl.BlockSpec((1, d), const)],
        out_specs=pl.BlockSpec((tm, d), rows),
        compiler_params=pltpu.CompilerParams(
            dimension_semantics=("parallel",), vmem_limit_bytes=VMEM_LIMIT),
        name="out_proj_ln",
    )(y_ssd, y_att, x2d, w1, w2, ln_g.astype(F32).reshape(1, -1), ln_b.astype(F32).reshape(1, -1))


FFN_HALO = 16


def _ffn_kernel(h_ref, hprev_ref, wg_ref, wu_ref, cwg_ref, cwu_ref, cbg_ref, cbu_ref, wd_ref,
                g_ref, b_ref, o_ref, hb_ref, acc_ref, *, tiles_per_seq):
    i = pl.program_id(0)
    j = pl.program_id(1)
    tm = h_ref.shape[0]

    @pl.when(j == 0)
    def _stage():
        starts_sequence = (i % tiles_per_seq) == 0
        hb_ref[0:FFN_HALO, :] = jnp.where(starts_sequence, 0.0, hprev_ref[...]).astype(BF16)
        hb_ref[FFN_HALO:, :] = h_ref[...].astype(BF16)
        acc_ref[...] = jnp.zeros_like(acc_ref)

    hb = hb_ref[...]

    def conv_branch(w_ref, cw_ref, cb_ref):
        u = jnp.dot(hb, w_ref[...], preferred_element_type=F32)
        c = cb_ref[...] + cw_ref[FFN_CONV - 1:FFN_CONV, :] * u
        for back in range(1, FFN_CONV):
            c = c + cw_ref[FFN_CONV - 1 - back:FFN_CONV - back, :] * pltpu.roll(u, back, 0)
        return c[FFN_HALO:]

    act = _silu(conv_branch(wg_ref, cwg_ref, cbg_ref)) * conv_branch(wu_ref, cwu_ref, cbu_ref)
    acc_ref[...] += jnp.dot(act.astype(BF16), wd_ref[...], preferred_element_type=F32)

    @pl.when(j == pl.num_programs(1) - 1)
    def _finish():
        o_ref[...] = _layer_norm_rows(DEEPNORM_ALPHA * h_ref[...] + acc_ref[...], g_ref[...], b_ref[...])


def _conv_ffn(h, seqlen, w_up, conv_w, conv_b, w_down, ln_g, ln_b, tm, n_split):
    n, d = h.shape
    d_ff = w_down.shape[0]
    tf = d_ff // n_split
    halo_blocks = tm // FFN_HALO
    w_up_b = w_up.astype(BF16)
    conv_b2 = conv_b.astype(F32).reshape(1, -1)
    rows = lambda i, j: (i, 0)
    const = lambda i, j: (0, 0)
    gate = lambda i, j: (0, j)
    up = lambda i, j: (0, n_split + j)
    return pl.pallas_call(
        functools.partial(_ffn_kernel, tiles_per_seq=seqlen // tm),
        out_shape=jax.ShapeDtypeStruct((n, d), F32),
        grid=(n // tm, n_split),
        in_specs=[pl.BlockSpec((tm, d), rows),
                  pl.BlockSpec((FFN_HALO, d), lambda i, j: (jnp.maximum(i * halo_blocks - 1, 0), 0)),
                  pl.BlockSpec((d, tf), gate), pl.BlockSpec((d, tf), up),
                  pl.BlockSpec((FFN_CONV, tf), gate), pl.BlockSpec((FFN_CONV, tf), up),
                  pl.BlockSpec((1, tf), gate), pl.BlockSpec((1, tf), up),
                  pl.BlockSpec((tf, d), lambda i, j: (j, 0)),
                  pl.BlockSpec((1, d), const), pl.BlockSpec((1, d), const)],
        out_specs=pl.BlockSpec((tm, d), rows),
        scratch_shapes=[pltpu.VMEM((tm + FFN_HALO, d), BF16), pltpu.VMEM((tm, d), F32)],
        compiler_params=pltpu.CompilerParams(
            dimension_semantics=("parallel", "arbitrary"), vmem_limit_bytes=VMEM_LIMIT),
        name="conv_ffn_ln",
    )(h, h, w_up_b, w_up_b, conv_w.astype(F32), conv_w.astype(F32), conv_b2, conv_b2,
      w_down.astype(BF16), ln_g.astype(F32).reshape(1, -1), ln_b.astype(F32).reshape(1, -1))


def _split_w_in(w_in):
    o = 0
    parts = {}
    for name, width in (("z", SSD_INNER), ("xbc", SSD_CONV_DIM), ("dt", SSD_HEADS), ("q", ATT_INNER),
                        ("k", ATT_KV_W), ("v", ATT_KV_W), ("qi", IDX_HEADS * IDX_DIM),
                        ("ki", IDX_DIM), ("wi", IDX_HEADS)):
        parts[name] = w_in[:, o:o + width]
        o += width
    d = w_in.shape[0]
    zeros = lambda w: jnp.zeros((d, w), w_in.dtype)
    wf = jnp.concatenate([parts["xbc"], parts["z"], parts["dt"], zeros(LANES - SSD_HEADS),
                          parts["ki"], parts["wi"], zeros(LANES - IDX_DIM - IDX_HEADS)], axis=1)
    q_scaled = parts["q"] * (ATT_HEAD_DIM ** -0.5 * LOG2E)
    wb = jnp.concatenate([q_scaled, parts["k"], parts["v"], parts["qi"]], axis=1)
    return wf.astype(BF16), wb.astype(BF16)


def _row_tile(n, seqlen, want):
    tm = min(want, seqlen)
    assert n % tm == 0 and seqlen % tm == 0
    return tm


def kernel(x, w_in, ssd_conv_w, ssd_conv_b, dt_bias, a_log, d_skip, ssd_norm_g, idx_k_norm_g,
           idx_k_norm_b, w_out, ln1_g, ln1_b, ffn_w_up, ffn_conv_w, ffn_conv_b, ffn_w_down, ln2_g, ln2_b):
    bsz, seqlen, d = x.shape
    n = bsz * seqlen
    assert seqlen % SSD_CHUNK == 0 and seqlen % Q_BLOCK == 0 and seqlen % min(KEY_TILE, seqlen) == 0
    h = x.reshape(n, d)
    for i in range(DEPTH):
        wf, wb = _split_w_in(w_in[i])
        tm = _row_tile(n, seqlen, 1024)
        pf = _project(h, wf, F32, tm, PF_WIDTH // 2)
        pb = _project(h, wb, BF16, tm, PB_WIDTH // 2)
        y_ssd = _ssd(pf, bsz, seqlen, ssd_conv_w[i], ssd_conv_b[i], dt_bias[i], a_log[i], d_skip[i],
                     ssd_norm_g[i])
        y_att = _dsa(pb, pf, bsz, seqlen, idx_k_norm_g[i], idx_k_norm_b[i])
        h1 = _out_proj(y_ssd, y_att, h, w_out[i], ln1_g[i], ln1_b[i], _row_tile(n, seqlen, 512))
        h = _conv_ffn(h1, seqlen, ffn_w_up[i], ffn_conv_w[i], ffn_conv_b[i], ffn_w_down[i],
                      ln2_g[i], ln2_b[i], _row_tile(n, seqlen, 512), 2)
    return h.reshape(bsz, seqlen, d)
```

```python
import functools

import jax
import jax.numpy as jnp
from jax import lax
from jax.experimental import pallas as pl
from jax.experimental.pallas import tpu as pltpu

F32 = jnp.float32
BF16 = jnp.bfloat16
I32 = jnp.int32
HIGHEST = lax.Precision.HIGHEST

SSD_HEADS = 16
SSD_HEAD_DIM = 64
SSD_INNER = SSD_HEADS * SSD_HEAD_DIM
SSD_GROUPS = 2
SSD_STATE = 128
SSD_BC = SSD_GROUPS * SSD_STATE
SSD_CONV = 4
SSD_CONV_DIM = SSD_INNER + 2 * SSD_BC
SSD_CHUNK = 128
GROUP_W = SSD_INNER // SSD_GROUPS
ATT_HEADS = 16
ATT_KV_HEADS = 4
ATT_HEAD_DIM = 64
ATT_INNER = ATT_HEADS * ATT_HEAD_DIM
ATT_KV_W = ATT_KV_HEADS * ATT_HEAD_DIM
N_REP = ATT_HEADS // ATT_KV_HEADS
IDX_HEADS = 8
IDX_DIM = 64
TOPK_MAX = 256
Q_BLOCK = 128
FFN_CONV = 3
LN_EPS = 1e-5
RMS_EPS = 1e-5
DEPTH = 1
DEEPNORM_ALPHA = (2 * DEPTH) ** 0.25

LANES = 128
VMEM_LIMIT = 48 * 1024 * 1024
KEY_TILE = 512
MASK_VALUE = -1e30
INT_MIN = -(2 ** 31)
LOG2E = 1.4426950408889634
FAST_BOUND = 60.0
V_ONES_ROWS = 16

PF_Z_BLK = SSD_CONV_DIM // GROUP_W
PF_DT_BLK = (SSD_CONV_DIM + SSD_INNER) // LANES
PF_IDX_BLK = PF_DT_BLK + 1
PF_WIDTH = SSD_CONV_DIM + SSD_INNER + 2 * LANES
PB_K_BLK = ATT_INNER // ATT_KV_W
PB_V_BLK = PB_K_BLK + 1
PB_QI_BLK = (ATT_INNER + 2 * ATT_KV_W) // (IDX_HEADS * IDX_DIM)
PB_WIDTH = ATT_INNER + 2 * ATT_KV_W + IDX_HEADS * IDX_DIM


def _layer_norm_rows(r, g, b):
    mu = jnp.mean(r, axis=-1, keepdims=True)
    d = r - mu
    var = jnp.mean(d * d, axis=-1, keepdims=True)
    return d * lax.rsqrt(var + LN_EPS) * g + b


def _silu(x):
    return x * jax.nn.sigmoid(x)


def _matmul_kernel(x_ref, w_ref, o_ref, xb_ref):
    @pl.when(pl.program_id(1) == 0)
    def _cast():
        xb_ref[...] = x_ref[...].astype(BF16)

    o_ref[...] = jnp.dot(xb_ref[...], w_ref[...], preferred_element_type=F32).astype(o_ref.dtype)


def _project(x2d, w, out_dtype, tm, tn):
    n, k = x2d.shape
    width = w.shape[1]
    return pl.pallas_call(
        _matmul_kernel,
        out_shape=jax.ShapeDtypeStruct((n, width), out_dtype),
        grid=(n // tm, width // tn),
        in_specs=[pl.BlockSpec((tm, k), lambda i, j: (i, 0)),
                  pl.BlockSpec((k, tn), lambda i, j: (0, j))],
        out_specs=pl.BlockSpec((tm, tn), lambda i, j: (i, j)),
        scratch_shapes=[pltpu.VMEM((tm, k), BF16)],
        compiler_params=pltpu.CompilerParams(
            dimension_semantics=("parallel", "arbitrary"), vmem_limit_bytes=VMEM_LIMIT),
        name="in_proj",
    )(x2d, w)


def _ssd_kernel(xbc_ref, z0_ref, z1_ref, dt_ref, cw_ref, cb_ref, dtb_ref, alog_ref, dsk_ref,
                g_ref, e_ref, tril_ref, y_ref, xpad_ref, st_ref):
    q = SSD_CHUNK
    halo = 8

    @pl.when(pl.program_id(1) == 0)
    def _init():
        xpad_ref[0:halo, :] = jnp.zeros((halo, SSD_CONV_DIM), F32)
        st_ref[...] = jnp.zeros_like(st_ref)

    xpad_ref[halo:halo + q, :] = xbc_ref[...]
    conv = cb_ref[...] + cw_ref[0:1, :] * xpad_ref[halo - 3:halo - 3 + q, :]
    for kk in range(1, SSD_CONV):
        conv = conv + cw_ref[kk:kk + 1, :] * xpad_ref[halo - 3 + kk:halo - 3 + kk + q, :]
    xpad_ref[0:halo, :] = xpad_ref[q:q + halo, :]
    xbc = _silu(conv)
    xs = xbc[:, :SSD_INNER]
    bm = xbc[:, SSD_INNER:SSD_INNER + SSD_BC]
    cm = xbc[:, SSD_INNER + SSD_BC:]

    dt_in = dt_ref[...] + dtb_ref[...]
    dtv = jnp.maximum(dt_in, 0.0) + jnp.log1p(jnp.exp(-jnp.abs(dt_in)))
    da = dtv * (-jnp.exp(alog_ref[...]))
    tril = tril_ref[...]
    acs = jnp.dot(tril, da, precision=HIGHEST, preferred_element_type=F32)
    acs_t = acs.T
    acs_last = acs[q - 1:q, :]
    dte = jnp.exp(acs_last - acs)
    eacs = jnp.exp(acs)
    cdec = jnp.broadcast_to(jnp.exp(acs_last), (8, LANES))
    stacked = jnp.concatenate([dtv, dte, eacs, cdec], axis=0)
    expanded = jnp.dot(stacked, e_ref[...], precision=HIGHEST, preferred_element_type=F32)
    dt_x = expanded[0:q]
    dte_x = expanded[q:2 * q]
    eacs_x = expanded[2 * q:3 * q]
    cdec_x = expanded[3 * q:3 * q + 1]

    xdt = xs * dt_x
    xdt_b = xdt.astype(BF16)
    wst_b = (xdt * dte_x).astype(BF16)
    causal = tril > 0.5
    lane = lax.broadcasted_iota(I32, (q, LANES), 1)
    lo_half = lane < SSD_HEAD_DIM
    heads_per_group = SSD_HEADS // SSD_GROUPS
    zs = (z0_ref, z1_ref)

    for g in range(SSD_GROUPS):
        gs = slice(g * GROUP_W, (g + 1) * GROUP_W)
        b_g = bm[:, g * SSD_STATE:(g + 1) * SSD_STATE]
        c_gb = cm[:, g * SSD_STATE:(g + 1) * SSD_STATE].astype(BF16)
        b_gb = b_g.astype(BF16)
        cb = lax.dot_general(c_gb, b_gb, (((1,), (1,)), ((), ())), preferred_element_type=F32)
        prev = st_ref[g]
        y_off = jnp.dot(c_gb, prev.astype(BF16), preferred_element_type=F32) * eacs_x[:, gs]
        st_ref[g] = prev * cdec_x[:, gs] + jnp.dot(b_g.T.astype(BF16), wst_b[:, gs],
                                                   preferred_element_type=F32)
        pairs = []
        for j in range(heads_per_group // 2):
            ms = []
            for h in (g * heads_per_group + 2 * j, g * heads_per_group + 2 * j + 1):
                seg = acs[:, h:h + 1] - acs_t[h:h + 1, :]
                lmat = jnp.where(causal, jnp.exp(seg), 0.0)
                ms.append((cb * lmat).astype(BF16))
            c0 = g * GROUP_W + 2 * j * SSD_HEAD_DIM
            xp = xdt_b[:, c0:c0 + LANES]
            zero = jnp.zeros_like(xp)
            rhs = jnp.concatenate([jnp.where(lo_half, xp, zero), jnp.where(lo_half, zero, xp)], axis=0)
            pairs.append(jnp.dot(jnp.concatenate(ms, axis=1), rhs, preferred_element_type=F32))
        y = jnp.concatenate(pairs, axis=1) + y_off + dsk_ref[:, gs] * xs[:, gs]
        gy = y * _silu(zs[g][...])
        ms2 = jnp.mean(gy * gy, axis=-1, keepdims=True)
        y_ref[:, gs] = (gy * lax.rsqrt(ms2 + RMS_EPS) * g_ref[:, gs]).astype(y_ref.dtype)


def _ssd(pf, bsz, seqlen, conv_w, conv_b, dt_bias, a_log, d_skip, norm_g):
    nc = seqlen // SSD_CHUNK
    q = SSD_CHUNK
    pad = LANES - SSD_HEADS
    dtb = jnp.pad(dt_bias.astype(F32), (0, pad)).reshape(1, LANES)
    alog = jnp.pad(a_log.astype(F32), (0, pad)).reshape(1, LANES)
    dsk = jnp.repeat(d_skip.astype(F32), SSD_HEAD_DIM).reshape(1, SSD_INNER)
    expand = (jnp.arange(LANES)[:, None] == (jnp.arange(SSD_INNER)[None, :] // SSD_HEAD_DIM)).astype(F32)
    tril = jnp.tril(jnp.ones((q, q), F32))
    row = lambda b, c: b * nc + c
    const = lambda b, c: (0, 0)
    return pl.pallas_call(
        _ssd_kernel,
        out_shape=jax.ShapeDtypeStruct((bsz * seqlen, SSD_INNER), BF16),
        grid=(bsz, nc),
        in_specs=[
            pl.BlockSpec((q, SSD_CONV_DIM), lambda b, c: (row(b, c), 0)),
            pl.BlockSpec((q, GROUP_W), lambda b, c: (row(b, c), PF_Z_BLK)),
            pl.BlockSpec((q, GROUP_W), lambda b, c: (row(b, c), PF_Z_BLK + 1)),
            pl.BlockSpec((q, LANES), lambda b, c: (row(b, c), PF_DT_BLK)),
            pl.BlockSpec((SSD_CONV, SSD_CONV_DIM), const),
            pl.BlockSpec((1, SSD_CONV_DIM), const),
            pl.BlockSpec((1, LANES), const),
            pl.BlockSpec((1, LANES), const),
            pl.BlockSpec((1, SSD_INNER), const),
            pl.BlockSpec((1, SSD_INNER), const),
            pl.BlockSpec((LANES, SSD_INNER), const),
            pl.BlockSpec((q, q), const),
        ],
        out_specs=pl.BlockSpec((q, SSD_INNER), lambda b, c: (row(b, c), 0)),
        scratch_shapes=[pltpu.VMEM((q + 8, SSD_CONV_DIM), F32),
                        pltpu.VMEM((SSD_GROUPS, SSD_STATE, GROUP_W), F32)],
        compiler_params=pltpu.CompilerParams(
            dimension_semantics=("parallel", "arbitrary"), vmem_limit_bytes=VMEM_LIMIT),
        name="ssd",
    )(pf, pf, pf, pf, conv_w.astype(F32), conv_b.astype(F32).reshape(1, -1), dtb, alog, dsk,
      norm_g.astype(F32).reshape(1, -1), expand, tril)


def _dsa_kernel(q_ref, k_ref, v_ref, qi_ref, idx_all_ref, idx_q_ref, ng_ref, nb_ref, o_ref,
                ki_s, k4_s, vt_s, key_s, qt_s, acc_s, m_s, s_s, kn_s, *, seqlen, topk):
    tk = min(KEY_TILE, seqlen)
    qb = pl.program_id(1)
    n_kt = (qb * Q_BLOCK + Q_BLOCK + tk - 1) // tk
    qpos = qb * Q_BLOCK + lax.broadcasted_iota(I32, (1, Q_BLOCK), 1)
    vrows = ATT_HEAD_DIM + V_ONES_ROWS

    def key_rows(kt):
        return pl.ds(pl.multiple_of(kt * tk, tk), tk)

    def key_pos(kt):
        return kt * tk + lax.broadcasted_iota(I32, (tk, 1), 0)

    @pl.when(qb == 0)
    def _prep():
        vt_s[...] = jnp.ones(vt_s.shape, BF16)
        kn_s[...] = jnp.zeros(kn_s.shape, F32)
        unit = (lax.broadcasted_iota(I32, (tk, LANES - ATT_HEAD_DIM), 1) == 0).astype(BF16)

        def body(i, carry):
            rows = key_rows(i)
            kin = idx_all_ref[rows, :][:, :IDX_DIM]
            ki_s[rows, :] = _layer_norm_rows(kin, ng_ref[...], nb_ref[...]).astype(BF16)
            kt_all = k_ref[rows, :]
            vt = v_ref[rows, :].astype(F32).T
            for n in range(ATT_KV_HEADS):
                k_n = kt_all[:, n * ATT_HEAD_DIM:(n + 1) * ATT_HEAD_DIM]
                k4_s[n, rows, :] = jnp.concatenate([k_n, unit], axis=1)
                k_f = k_n.astype(F32)
                norm2 = jnp.max(jnp.sum(k_f * k_f, axis=1, keepdims=True), axis=0, keepdims=True)
                kn_s[n:n + 1, :] = jnp.maximum(kn_s[n:n + 1, :], norm2)
                vt_s[n * vrows:n * vrows + ATT_HEAD_DIM, rows] = (
                    vt[n * ATT_HEAD_DIM:(n + 1) * ATT_HEAD_DIM].astype(BF16))
            return carry
        lax.fori_loop(0, seqlen // tk, body, 0)

    w_t = idx_q_ref[...].T[IDX_DIM:IDX_DIM + IDX_HEADS, :] * (IDX_HEADS ** -0.5) * (IDX_DIM ** -0.5)
    qi_t = qi_ref[...].astype(F32).T.astype(BF16)
    qi_rhs = jnp.concatenate([qi_t[h * IDX_DIM:(h + 1) * IDX_DIM] for h in range(IDX_HEADS)], axis=1)

    def score_tile(kt, carry):
        rows = key_rows(kt)
        s_all = jnp.dot(ki_s[rows, :], qi_rhs, preferred_element_type=F32)
        acc = jnp.zeros((tk, Q_BLOCK), F32)
        for h in range(IDX_HEADS):
            acc = acc + jnp.maximum(s_all[:, h * Q_BLOCK:(h + 1) * Q_BLOCK], 0.0) * w_t[h:h + 1, :]
        bits = pltpu.bitcast(acc, I32)
        keys = bits ^ ((bits >> 31) & jnp.int32(0x7FFFFFFF))
        keys = jnp.where(key_pos(kt) <= qpos, keys, INT_MIN)
        key_s[rows, :] = keys
        return carry

    lax.fori_loop(0, n_kt, score_tile, 0)

    def count_keys(pred):
        def body(kt, cnt):
            m = pred(key_s[key_rows(kt), :], kt).astype(I32)
            return cnt + jnp.sum(m.reshape(tk // 8, 8, Q_BLOCK), axis=0)
        cnt = lax.fori_loop(0, n_kt, body, jnp.zeros((8, Q_BLOCK), I32))
        return jnp.sum(cnt, axis=0, keepdims=True)

    def bit_step(i, t_u):
        cand_u = t_u | jnp.left_shift(jnp.int32(1), 31 - i)
        cand_s = cand_u ^ INT_MIN
        cnt = count_keys(lambda keys, kt: keys >= cand_s)
        return jnp.where(cnt >= topk, cand_u, t_u)

    thr = lax.fori_loop(0, 32, bit_step, jnp.zeros((1, Q_BLOCK), I32)) ^ INT_MIN
    n_gt = count_keys(lambda keys, kt: keys > thr)
    n_eq = count_keys(lambda keys, kt: keys == thr)
    need = topk - n_gt

    @pl.when(jnp.max(jnp.where((n_eq > need) & (thr != INT_MIN), 1, 0)) > 0)
    def _break_ties():
        nbits = (seqlen - 1).bit_length()

        def idx_step(i, p):
            cand = p | jnp.left_shift(jnp.int32(1), nbits - 1 - i)
            cnt = count_keys(lambda keys, kt: (keys == thr) & (key_pos(kt) < cand))
            return jnp.where(cnt < need, cand, p)
        cut = lax.fori_loop(0, nbits, idx_step, jnp.zeros((1, Q_BLOCK), I32))
        below = jnp.maximum(thr, INT_MIN + 1) - 1

        def demote(kt, carry):
            rows = key_rows(kt)
            keys = key_s[rows, :]
            key_s[rows, :] = jnp.where((keys == thr) & (key_pos(kt) > cut), below, keys)
            return carry
        lax.fori_loop(0, n_kt, demote, 0)

    thr_sel = jnp.where(thr == INT_MIN, INT_MIN + 1, thr)

    lanes4 = N_REP * Q_BLOCK
    q_f = q_ref[...].astype(F32).T
    row0 = lax.broadcasted_iota(I32, (LANES - ATT_HEAD_DIM, lanes4), 0) == 0
    ref_max = jnp.zeros((1, lanes4), F32)
    for n in range(ATT_KV_HEADS):
        blocks = [q_f[(n * N_REP + g) * ATT_HEAD_DIM:(n * N_REP + g + 1) * ATT_HEAD_DIM] for g in range(N_REP)]
        q_n = jnp.concatenate(blocks, axis=1)
        ref = (jnp.sqrt(jnp.sum(q_n * q_n, axis=0, keepdims=True) * kn_s[n:n + 1, 0:1]) * 1.01).astype(BF16)
        ref_max = jnp.maximum(ref_max, ref.astype(F32))
        extra = jnp.where(row0, -ref.astype(F32), 0.0)
        qt_s[n] = jnp.concatenate([q_n, extra], axis=0).astype(BF16)
    acc_s[...] = jnp.zeros(acc_s.shape, F32)
    fast = jnp.max(ref_max) <= FAST_BOUND

    @pl.when(fast)
    def _fixed_reference():
        def tile(kt, carry):
            rows = key_rows(kt)
            mask = jnp.where(key_s[rows, :] >= thr_sel, 1.0, 0.0).astype(BF16)
            mask4 = jnp.concatenate([mask] * N_REP, axis=1)
            def qk(n):
                return jnp.dot(k4_s[n, rows, :], qt_s[n], preferred_element_type=F32)

            def pv(n, s):
                p = jnp.exp2(s).astype(BF16) * mask4
                acc_s[n] += jnp.dot(vt_s[n * vrows:(n + 1) * vrows, rows], p, preferred_element_type=F32)

            s_cur = qk(0)
            for n in range(1, ATT_KV_HEADS):
                s_nxt = qk(n)
                pv(n - 1, s_cur)
                s_cur = s_nxt
            pv(ATT_KV_HEADS - 1, s_cur)
            return carry
        lax.fori_loop(0, n_kt, tile, 0)

    @pl.when(jnp.logical_not(fast))
    def _running_maximum():
        for n in range(ATT_KV_HEADS):
            qt_s[n, ATT_HEAD_DIM:, :] = jnp.zeros((LANES - ATT_HEAD_DIM, lanes4), BF16)
        m_s[...] = jnp.full(m_s.shape, MASK_VALUE, F32)

        def tile(kt, carry):
            rows = key_rows(kt)
            bias = jnp.where(key_s[rows, :] >= thr_sel, 0.0, MASK_VALUE)
            bias4 = jnp.concatenate([bias] * N_REP, axis=1)
            m_new = []
            for n in range(ATT_KV_HEADS):
                s = jnp.dot(k4_s[n, rows, :], qt_s[n], preferred_element_type=F32) + bias4
                s_s[n] = s
                tile_max = jnp.max(jnp.max(s.reshape(tk // 8, 8, lanes4), axis=0), axis=0, keepdims=True)
                m_new.append(jnp.maximum(m_s[n], tile_max))
            for n in range(ATT_KV_HEADS):
                alpha = jnp.exp2(m_s[n] - m_new[n])
                p = jnp.exp2(s_s[n] - m_new[n]).astype(BF16)
                acc_s[n] = alpha * acc_s[n] + jnp.dot(vt_s[n * vrows:(n + 1) * vrows, rows], p,
                                                      preferred_element_type=F32)
                m_s[n] = m_new[n]
            return carry
        lax.fori_loop(0, n_kt, tile, 0)

    heads = []
    for n in range(ATT_KV_HEADS):
        a = acc_s[n]
        o_n = a[:ATT_HEAD_DIM] / a[ATT_HEAD_DIM:ATT_HEAD_DIM + 1]
        heads += [o_n[:, g * Q_BLOCK:(g + 1) * Q_BLOCK] for g in range(N_REP)]
    o_ref[...] = jnp.concatenate(heads, axis=0).T.astype(o_ref.dtype)


def _dsa(pb, pf, bsz, seqlen, norm_g, norm_b):
    nb = seqlen // Q_BLOCK
    topk = min(TOPK_MAX, seqlen // 4)
    row = lambda b, i: b * nb + i
    const = lambda b, i: (0, 0)
    lanes4 = N_REP * Q_BLOCK
    vrows = ATT_HEAD_DIM + V_ONES_ROWS
    return pl.pallas_call(
        functools.partial(_dsa_kernel, seqlen=seqlen, topk=topk),
        out_shape=jax.ShapeDtypeStruct((bsz * seqlen, ATT_INNER), BF16),
        grid=(bsz, nb),
        in_specs=[
            pl.BlockSpec((Q_BLOCK, ATT_INNER), lambda b, i: (row(b, i), 0)),
            pl.BlockSpec((seqlen, ATT_KV_W), lambda b, i: (b, PB_K_BLK)),
            pl.BlockSpec((seqlen, ATT_KV_W), lambda b, i: (b, PB_V_BLK)),
            pl.BlockSpec((Q_BLOCK, IDX_HEADS * IDX_DIM), lambda b, i: (row(b, i), PB_QI_BLK)),
            pl.BlockSpec((seqlen, LANES), lambda b, i: (b, PF_IDX_BLK)),
            pl.BlockSpec((Q_BLOCK, LANES), lambda b, i: (row(b, i), PF_IDX_BLK)),
            pl.BlockSpec((1, IDX_DIM), const),
            pl.BlockSpec((1, IDX_DIM), const),
        ],
        out_specs=pl.BlockSpec((Q_BLOCK, ATT_INNER), lambda b, i: (row(b, i), 0)),
        scratch_shapes=[
            pltpu.VMEM((seqlen, IDX_DIM), BF16),
            pltpu.VMEM((ATT_KV_HEADS, seqlen, LANES), BF16),
            pltpu.VMEM((ATT_KV_HEADS * vrows, seqlen), BF16),
            pltpu.VMEM((seqlen, Q_BLOCK), I32),
            pltpu.VMEM((ATT_KV_HEADS, LANES, lanes4), BF16),
            pltpu.VMEM((ATT_KV_HEADS, vrows, lanes4), F32),
            pltpu.VMEM((ATT_KV_HEADS, 1, lanes4), F32),
            pltpu.VMEM((ATT_KV_HEADS, min(KEY_TILE, seqlen), lanes4), F32),
            pltpu.VMEM((8, LANES), F32),
        ],
        compiler_params=pltpu.CompilerParams(
            dimension_semantics=("parallel", "arbitrary"), vmem_limit_bytes=VMEM_LIMIT),
        name="dsa",
    )(pb, pb, pb, pb, pf, pf, norm_g.astype(F32).reshape(1, -1), norm_b.astype(F32).reshape(1, -1))


def _outproj_kernel(ys_ref, ya_ref, x_ref, w1_ref, w2_ref, g_ref, b_ref, h_ref):
    m = jnp.dot(ys_ref[...], w1_ref[...], preferred_element_type=F32)
    m = m + jnp.dot(ya_ref[...], w2_ref[...], preferred_element_type=F32)
    h_ref[...] = _layer_norm_rows(DEEPNORM_ALPHA * x_ref[...] + m, g_ref[...], b_ref[...])


def _out_proj(y_ssd, y_att, x2d, w_out, ln_g, ln_b, tm):
    n, d = x2d.shape
    w1 = w_out[:SSD_INNER].astype(BF16)
    w2 = w_out[SSD_INNER:].astype(BF16)
    rows = lambda i: (i, 0)
    const = lambda i: (0, 0)
    return pl.pallas_call(
        _outproj_kernel,
        out_shape=jax.ShapeDtypeStruct((n, d), F32),
        grid=(n // tm,),
        in_specs=[pl.BlockSpec((tm, SSD_INNER), rows), pl.BlockSpec((tm, ATT_INNER), rows),
                  pl.BlockSpec((tm, d), rows),
                  pl.BlockSpec((SSD_INNER, d), const), pl.BlockSpec((ATT_INNER, d), const),
                  pl.BlockSpec((1, d), const), pl.BlockSpec((1, d), const)],
        out_specs=pl.BlockSpec((tm, d), rows),
        compiler_params=pltpu.CompilerParams(
            dimension_semantics=("parallel",), vmem_limit_bytes=VMEM_LIMIT),
        name="out_proj_ln",
    )(y_ssd, y_att, x2d, w1, w2, ln_g.astype(F32).reshape(1, -1), ln_b.astype(F32).reshape(1, -1))


FFN_HALO = 16


def _ffn_kernel(h_ref, hprev_ref, wg_ref, wu_ref, cwg_ref, cwu_ref, cbg_ref, cbu_ref, wd_ref,
                g_ref, b_ref, o_ref, hb_ref, acc_ref, *, tiles_per_seq):
    i = pl.program_id(0)
    j = pl.program_id(1)
    tm = h_ref.shape[0]

    @pl.when(j == 0)
    def _stage():
        starts_sequence = (i % tiles_per_seq) == 0
        hb_ref[0:FFN_HALO, :] = jnp.where(starts_sequence, 0.0, hprev_ref[...]).astype(BF16)
        hb_ref[FFN_HALO:, :] = h_ref[...].astype(BF16)
        acc_ref[...] = jnp.zeros_like(acc_ref)

    hb = hb_ref[...]

    def conv_branch(w_ref, cw_ref, cb_ref):
        u = jnp.dot(hb, w_ref[...], preferred_element_type=F32)
        c = cb_ref[...] + cw_ref[FFN_CONV - 1:FFN_CONV, :] * u
        for back in range(1, FFN_CONV):
            c = c + cw_ref[FFN_CONV - 1 - back:FFN_CONV - back, :] * pltpu.roll(u, back, 0)
        return c[FFN_HALO:]

    act = _silu(conv_branch(wg_ref, cwg_ref, cbg_ref)) * conv_branch(wu_ref, cwu_ref, cbu_ref)
    acc_ref[...] += jnp.dot(act.astype(BF16), wd_ref[...], preferred_element_type=F32)

    @pl.when(j == pl.num_programs(1) - 1)
    def _finish():
        o_ref[...] = _layer_norm_rows(DEEPNORM_ALPHA * h_ref[...] + acc_ref[...], g_ref[...], b_ref[...])


def _conv_ffn(h, seqlen, w_up, conv_w, conv_b, w_down, ln_g, ln_b, tm, n_split):
    n, d = h.shape
    d_ff = w_down.shape[0]
    tf = d_ff // n_split
    halo_blocks = tm // FFN_HALO
    w_up_b = w_up.astype(BF16)
    conv_b2 = conv_b.astype(F32).reshape(1, -1)
    rows = lambda i, j: (i, 0)
    const = lambda i, j: (0, 0)
    gate = lambda i, j: (0, j)
    up = lambda i, j: (0, n_split + j)
    return pl.pallas_call(
        functools.partial(_ffn_kernel, tiles_per_seq=seqlen // tm),
        out_shape=jax.ShapeDtypeStruct((n, d), F32),
        grid=(n // tm, n_split),
        in_specs=[pl.BlockSpec((tm, d), rows),
                  pl.BlockSpec((FFN_HALO, d), lambda i, j: (jnp.maximum(i * halo_blocks - 1, 0), 0)),
                  pl.BlockSpec((d, tf), gate), pl.BlockSpec((d, tf), up),
                  pl.BlockSpec((FFN_CONV, tf), gate), pl.BlockSpec((FFN_CONV, tf), up),
                  pl.BlockSpec((1, tf), gate), pl.BlockSpec((1, tf), up),
                  pl.BlockSpec((tf, d), lambda i, j: (j, 0)),
                  pl.BlockSpec((1, d), const), pl.BlockSpec((1, d), const)],
        out_specs=pl.BlockSpec((tm, d), rows),
        scratch_shapes=[pltpu.VMEM((tm + FFN_HALO, d), BF16), pltpu.VMEM((tm, d), F32)],
        compiler_params=pltpu.CompilerParams(
            dimension_semantics=("parallel", "arbitrary"), vmem_limit_bytes=VMEM_LIMIT),
        name="conv_ffn_ln",
    )(h, h, w_up_b, w_up_b, conv_w.astype(F32), conv_w.astype(F32), conv_b2, conv_b2,
      w_down.astype(BF16), ln_g.astype(F32).reshape(1, -1), ln_b.astype(F32).reshape(1, -1))


def _split_w_in(w_in):
    o = 0
    parts = {}
    for name, width in (("z", SSD_INNER), ("xbc", SSD_CONV_DIM), ("dt", SSD_HEADS), ("q", ATT_INNER),
                        ("k", ATT_KV_W), ("v", ATT_KV_W), ("qi", IDX_HEADS * IDX_DIM),
                        ("ki", IDX_DIM), ("wi", IDX_HEADS)):
        parts[name] = w_in[:, o:o + width]
        o += width
    d = w_in.shape[0]
    zeros = lambda w: jnp.zeros((d, w), w_in.dtype)
    wf = jnp.concatenate([parts["xbc"], parts["z"], parts["dt"], zeros(LANES - SSD_HEADS),
                          parts["ki"], parts["wi"], zeros(LANES - IDX_DIM - IDX_HEADS)], axis=1)
    q_scaled = parts["q"] * (ATT_HEAD_DIM ** -0.5 * LOG2E)
    wb = jnp.concatenate([q_scaled, parts["k"], parts["v"], parts["qi"]], axis=1)
    return wf.astype(BF16), wb.astype(BF16)


def _row_tile(n, seqlen, want):
    tm = min(want, seqlen)
    assert n % tm == 0 and seqlen % tm == 0
    return tm


def kernel(x, w_in, ssd_conv_w, ssd_conv_b, dt_bias, a_log, d_skip, ssd_norm_g, idx_k_norm_g,
           idx_k_norm_b, w_out, ln1_g, ln1_b, ffn_w_up, ffn_conv_w, ffn_conv_b, ffn_w_down, ln2_g, ln2_b):
    bsz, seqlen, d = x.shape
    n = bsz * seqlen
    assert seqlen % SSD_CHUNK == 0 and seqlen % Q_BLOCK == 0 and seqlen % min(KEY_TILE, seqlen) == 0
    h = x.reshape(n, d)
    for i in range(DEPTH):
        wf, wb = _split_w_in(w_in[i])
        tm = _row_tile(n, seqlen, 1024)
        pf = _project(h, wf, F32, tm, PF_WIDTH // 2)
        pb = _project(h, wb, BF16, tm, PB_WIDTH // 2)
        y_ssd = _ssd(pf, bsz, seqlen, ssd_conv_w[i], ssd_conv_b[i], dt_bias[i], a_log[i], d_skip[i],
                     ssd_norm_g[i])
        y_att = _dsa(pb, pf, bsz, seqlen, idx_k_norm_g[i], idx_k_norm_b[i])
        h1 = _out_proj(y_ssd, y_att, h, w_out[i], ln1_g[i], ln1_b[i], _row_tile(n, seqlen, 512))
        h = _conv_ffn(h1, seqlen, ffn_w_up[i], ffn_conv_w[i], ffn_conv_b[i], ffn_w_down[i],
                      ln2_g[i], ln2_b[i], _row_tile(n, seqlen, 512), 2)
    return h.reshape(bsz, seqlen, d)
```

```python
import functools

import jax
import jax.numpy as jnp
from jax import lax
from jax.experimental import pallas as pl
from jax.experimental.pallas import tpu as pltpu

F32 = jnp.float32
BF16 = jnp.bfloat16
I32 = jnp.int32
HIGHEST = lax.Precision.HIGHEST

SSD_HEADS = 16
SSD_HEAD_DIM = 64
SSD_INNER = SSD_HEADS * SSD_HEAD_DIM
SSD_GROUPS = 2
SSD_STATE = 128
SSD_BC = SSD_GROUPS * SSD_STATE
SSD_CONV = 4
SSD_CONV_DIM = SSD_INNER + 2 * SSD_BC
SSD_CHUNK = 128
GROUP_W = SSD_INNER // SSD_GROUPS
ATT_HEADS = 16
ATT_KV_HEADS = 4
ATT_HEAD_DIM = 64
ATT_INNER = ATT_HEADS * ATT_HEAD_DIM
ATT_KV_W = ATT_KV_HEADS * ATT_HEAD_DIM
N_REP = ATT_HEADS // ATT_KV_HEADS
IDX_HEADS = 8
IDX_DIM = 64
TOPK_MAX = 256
Q_BLOCK = 128
FFN_CONV = 3
LN_EPS = 1e-5
RMS_EPS = 1e-5
DEPTH = 1
DEEPNORM_ALPHA = (2 * DEPTH) ** 0.25

LANES = 128
VMEM_LIMIT = 48 * 1024 * 1024
KEY_TILE = 512
MASK_VALUE = -1e30
INT_MIN = -(2 ** 31)
LOG2E = 1.4426950408889634
FAST_BOUND = 60.0
V_ONES_ROWS = 16

PF_Z_BLK = SSD_CONV_DIM // GROUP_W
PF_DT_BLK = (SSD_CONV_DIM + SSD_INNER) // LANES
PF_IDX_BLK = PF_DT_BLK + 1
PF_WIDTH = SSD_CONV_DIM + SSD_INNER + 2 * LANES
PB_K_BLK = ATT_INNER // ATT_KV_W
PB_V_BLK = PB_K_BLK + 1
PB_QI_BLK = (ATT_INNER + 2 * ATT_KV_W) // (IDX_HEADS * IDX_DIM)
PB_WIDTH = ATT_INNER + 2 * ATT_KV_W + IDX_HEADS * IDX_DIM


def _layer_norm_rows(r, g, b):
    mu = jnp.mean(r, axis=-1, keepdims=True)
    d = r - mu
    var = jnp.mean(d * d, axis=-1, keepdims=True)
    return d * lax.rsqrt(var + LN_EPS) * g + b


def _silu(x):
    return x * jax.nn.sigmoid(x)


def _matmul_kernel(x_ref, w_ref, o_ref, xb_ref):
    @pl.when(pl.program_id(1) == 0)
    def _cast():
        xb_ref[...] = x_ref[...].astype(BF16)

    o_ref[...] = jnp.dot(xb_ref[...], w_ref[...], preferred_element_type=F32).astype(o_ref.dtype)


def _project(x2d, w, out_dtype, tm, tn):
    n, k = x2d.shape
    width = w.shape[1]
    return pl.pallas_call(
        _matmul_kernel,
        out_shape=jax.ShapeDtypeStruct((n, width), out_dtype),
        grid=(n // tm, width // tn),
        in_specs=[pl.BlockSpec((tm, k), lambda i, j: (i, 0)),
                  pl.BlockSpec((k, tn), lambda i, j: (0, j))],
        out_specs=pl.BlockSpec((tm, tn), lambda i, j: (i, j)),
        scratch_shapes=[pltpu.VMEM((tm, k), BF16)],
        compiler_params=pltpu.CompilerParams(
            dimension_semantics=("parallel", "arbitrary"), vmem_limit_bytes=VMEM_LIMIT),
        name="in_proj",
    )(x2d, w)


def _ssd_kernel(xbc_ref, z0_ref, z1_ref, dt_ref, cw_ref, cb_ref, dtb_ref, alog_ref, dsk_ref,
                g_ref, e_ref, tril_ref, y_ref, xpad_ref, st_ref):
    q = SSD_CHUNK
    halo = 8

    @pl.when(pl.program_id(1) == 0)
    def _init():
        xpad_ref[0:halo, :] = jnp.zeros((halo, SSD_CONV_DIM), F32)
        st_ref[...] = jnp.zeros_like(st_ref)

    xpad_ref[halo:halo + q, :] = xbc_ref[...]
    xp = xpad_ref[...]
    conv = cb_ref[...] + cw_ref[SSD_CONV - 1:SSD_CONV, :] * xp[halo:]
    for back in range(1, SSD_CONV):
        conv = conv + cw_ref[SSD_CONV - 1 - back:SSD_CONV - back, :] * pltpu.roll(xp, back, 0)[halo:]
    xpad_ref[0:halo, :] = xpad_ref[q:q + halo, :]
    xbc = _silu(conv)
    xs = xbc[:, :SSD_INNER]
    bm = xbc[:, SSD_INNER:SSD_INNER + SSD_BC]
    cm = xbc[:, SSD_INNER + SSD_BC:]

    dt_in = dt_ref[...] + dtb_ref[...]
    dtv = jnp.maximum(dt_in, 0.0) + jnp.log1p(jnp.exp(-jnp.abs(dt_in)))
    da = dtv * (-jnp.exp(alog_ref[...]))
    tril = tril_ref[...]
    acs = jnp.dot(tril, da, precision=HIGHEST, preferred_element_type=F32)
    acs_t = acs.T
    acs_last = acs[q - 1:q, :]
    dte = jnp.exp(acs_last - acs)
    eacs = jnp.exp(acs)
    cdec = jnp.broadcast_to(jnp.exp(acs_last), (8, LANES))
    stacked = jnp.concatenate([dtv, dte, eacs, cdec], axis=0)
    expanded = jnp.dot(stacked, e_ref[...], precision=HIGHEST, preferred_element_type=F32)
    dt_x = expanded[0:q]
    dte_x = expanded[q:2 * q]
    eacs_x = expanded[2 * q:3 * q]
    cdec_x = expanded[3 * q:3 * q + 1]

    xdt = xs * dt_x
    xdt_b = xdt.astype(BF16)
    wst_b = (xdt * dte_x).astype(BF16)
    causal = tril > 0.5
    lane = lax.broadcasted_iota(I32, (q, LANES), 1)
    lo_half = lane < SSD_HEAD_DIM
    heads_per_group = SSD_HEADS // SSD_GROUPS
    zs = (z0_ref, z1_ref)

    for g in range(SSD_GROUPS):
        gs = slice(g * GROUP_W, (g + 1) * GROUP_W)
        b_g = bm[:, g * SSD_STATE:(g + 1) * SSD_STATE]
        c_gb = cm[:, g * SSD_STATE:(g + 1) * SSD_STATE].astype(BF16)
        b_gb = b_g.astype(BF16)
        cb = lax.dot_general(c_gb, b_gb, (((1,), (1,)), ((), ())), preferred_element_type=F32)
        prev = st_ref[g]
        y_off = jnp.dot(c_gb, prev.astype(BF16), preferred_element_type=F32) * eacs_x[:, gs]
        st_ref[g] = prev * cdec_x[:, gs] + jnp.dot(b_g.T.astype(BF16), wst_b[:, gs],
                                                   preferred_element_type=F32)
        pairs = []
        for j in range(heads_per_group // 2):
            ms = []
            for h in (g * heads_per_group + 2 * j, g * heads_per_group + 2 * j + 1):
                seg = acs[:, h:h + 1] - acs_t[h:h + 1, :]
                lmat = jnp.where(causal, jnp.exp(seg), 0.0)
                ms.append((cb * lmat).astype(BF16))
            c0 = g * GROUP_W + 2 * j * SSD_HEAD_DIM
            xp = xdt_b[:, c0:c0 + LANES]
            zero = jnp.zeros_like(xp)
            rhs = jnp.concatenate([jnp.where(lo_half, xp, zero), jnp.where(lo_half, zero, xp)], axis=0)
            pairs.append(jnp.dot(jnp.concatenate(ms, axis=1), rhs, preferred_element_type=F32))
        y = jnp.concatenate(pairs, axis=1) + y_off + dsk_ref[:, gs] * xs[:, gs]
        gy = y * _silu(zs[g][...])
        ms2 = jnp.mean(gy * gy, axis=-1, keepdims=True)
        y_ref[:, gs] = (gy * lax.rsqrt(ms2 + RMS_EPS) * g_ref[:, gs]).astype(y_ref.dtype)


def _ssd(pf, bsz, seqlen, conv_w, conv_b, dt_bias, a_log, d_skip, norm_g):
    nc = seqlen // SSD_CHUNK
    q = SSD_CHUNK
    pad = LANES - SSD_HEADS
    dtb = jnp.pad(dt_bias.astype(F32), (0, pad)).reshape(1, LANES)
    alog = jnp.pad(a_log.astype(F32), (0, pad)).reshape(1, LANES)
    dsk = jnp.repeat(d_skip.astype(F32), SSD_HEAD_DIM).reshape(1, SSD_INNER)
    expand = (jnp.arange(LANES)[:, None] == (jnp.arange(SSD_INNER)[None, :] // SSD_HEAD_DIM)).astype(F32)
    tril = jnp.tril(jnp.ones((q, q), F32))
    row = lambda b, c: b * nc + c
    const = lambda b, c: (0, 0)
    return pl.pallas_call(
        _ssd_kernel,
        out_shape=jax.ShapeDtypeStruct((bsz * seqlen, SSD_INNER), BF16),
        grid=(bsz, nc),
        in_specs=[
            pl.BlockSpec((q, SSD_CONV_DIM), lambda b, c: (row(b, c), 0)),
            pl.BlockSpec((q, GROUP_W), lambda b, c: (row(b, c), PF_Z_BLK)),
            pl.BlockSpec((q, GROUP_W), lambda b, c: (row(b, c), PF_Z_BLK + 1)),
            pl.BlockSpec((q, LANES), lambda b, c: (row(b, c), PF_DT_BLK)),
            pl.BlockSpec((SSD_CONV, SSD_CONV_DIM), const),
            pl.BlockSpec((1, SSD_CONV_DIM), const),
            pl.BlockSpec((1, LANES), const),
            pl.BlockSpec((1, LANES), const),
            pl.BlockSpec((1, SSD_INNER), const),
            pl.BlockSpec((1, SSD_INNER), const),
            pl.BlockSpec((LANES, SSD_INNER), const),
            pl.BlockSpec((q, q), const),
        ],
        out_specs=pl.BlockSpec((q, SSD_INNER), lambda b, c: (row(b, c), 0)),
        scratch_shapes=[pltpu.VMEM((q + 8, SSD_CONV_DIM), F32),
                        pltpu.VMEM((SSD_GROUPS, SSD_STATE, GROUP_W), F32)],
        compiler_params=pltpu.CompilerParams(
            dimension_semantics=("parallel", "arbitrary"), vmem_limit_bytes=VMEM_LIMIT),
        name="ssd",
    )(pf, pf, pf, pf, conv_w.astype(F32), conv_b.astype(F32).reshape(1, -1), dtb, alog, dsk,
      norm_g.astype(F32).reshape(1, -1), expand, tril)


def _dsa_kernel(q_ref, k_ref, v_ref, qi_ref, idx_all_ref, idx_q_ref, ng_ref, nb_ref, o_ref,
                ki_s, k4_s, vt_s, key_s, qt_s, acc_s, m_s, s_s, kn_s, *, seqlen, topk):
    tk = min(KEY_TILE, seqlen)
    qb = pl.program_id(1)
    n_kt = (qb * Q_BLOCK + Q_BLOCK + tk - 1) // tk
    qpos = qb * Q_BLOCK + lax.broadcasted_iota(I32, (1, Q_BLOCK), 1)
    vrows = ATT_HEAD_DIM + V_ONES_ROWS

    def key_rows(kt):
        return pl.ds(pl.multiple_of(kt * tk, tk), tk)

    def key_pos(kt):
        return kt * tk + lax.broadcasted_iota(I32, (tk, 1), 0)

    @pl.when(qb == 0)
    def _prep():
        vt_s[...] = jnp.ones(vt_s.shape, BF16)
        kn_s[...] = jnp.zeros(kn_s.shape, F32)
        unit = (lax.broadcasted_iota(I32, (tk, LANES - ATT_HEAD_DIM), 1) == 0).astype(BF16)

        def body(i, carry):
            rows = key_rows(i)
            kin = idx_all_ref[rows, :][:, :IDX_DIM]
            ki_s[rows, :] = _layer_norm_rows(kin, ng_ref[...], nb_ref[...]).astype(BF16)
            kt_all = k_ref[rows, :]
            vt = v_ref[rows, :].astype(F32).T
            for n in range(ATT_KV_HEADS):
                k_n = kt_all[:, n * ATT_HEAD_DIM:(n + 1) * ATT_HEAD_DIM]
                k4_s[n, rows, :] = jnp.concatenate([k_n, unit], axis=1)
                k_f = k_n.astype(F32)
                norm2 = jnp.max(jnp.sum(k_f * k_f, axis=1, keepdims=True), axis=0, keepdims=True)
                kn_s[n:n + 1, :] = jnp.maximum(kn_s[n:n + 1, :], norm2)
                vt_s[n * vrows:n * vrows + ATT_HEAD_DIM, rows] = (
                    vt[n * ATT_HEAD_DIM:(n + 1) * ATT_HEAD_DIM].astype(BF16))
            return carry
        lax.fori_loop(0, seqlen // tk, body, 0)

    w_t = idx_q_ref[...].T[IDX_DIM:IDX_DIM + IDX_HEADS, :] * (IDX_HEADS ** -0.5) * (IDX_DIM ** -0.5)
    qi_t = qi_ref[...].astype(F32).T.astype(BF16)
    qi_rhs = jnp.concatenate([qi_t[h * IDX_DIM:(h + 1) * IDX_DIM] for h in range(IDX_HEADS)], axis=1)

    def score_tile(kt, carry):
        rows = key_rows(kt)
        s_all = jnp.dot(ki_s[rows, :], qi_rhs, preferred_element_type=F32)
        acc = jnp.zeros((tk, Q_BLOCK), F32)
        for h in range(IDX_HEADS):
            acc = acc + jnp.maximum(s_all[:, h * Q_BLOCK:(h + 1) * Q_BLOCK], 0.0) * w_t[h:h + 1, :]
        bits = pltpu.bitcast(acc, I32)
        keys = bits ^ ((bits >> 31) & jnp.int32(0x7FFFFFFF))
        keys = jnp.where(key_pos(kt) <= qpos, keys, INT_MIN)
        key_s[rows, :] = keys
        return carry

    lax.fori_loop(0, n_kt, score_tile, 0)

    def count_keys(pred):
        def body(kt, cnt):
            m = pred(key_s[key_rows(kt), :], kt).astype(I32)
            return cnt + jnp.sum(m.reshape(tk // 8, 8, Q_BLOCK), axis=0)
        cnt = lax.fori_loop(0, n_kt, body, jnp.zeros((8, Q_BLOCK), I32))
        return jnp.sum(cnt, axis=0, keepdims=True)

    def bit_step(i, t_u):
        cand_u = t_u | jnp.left_shift(jnp.int32(1), 31 - i)
        cand_s = cand_u ^ INT_MIN
        cnt = count_keys(lambda keys, kt: keys >= cand_s)
        return jnp.where(cnt >= topk, cand_u, t_u)

    thr = lax.fori_loop(0, 32, bit_step, jnp.zeros((1, Q_BLOCK), I32)) ^ INT_MIN
    n_gt = count_keys(lambda keys, kt: keys > thr)
    n_eq = count_keys(lambda keys, kt: keys == thr)
    need = topk - n_gt

    @pl.when(jnp.max(jnp.where((n_eq > need) & (thr != INT_MIN), 1, 0)) > 0)
    def _break_ties():
        nbits = (seqlen - 1).bit_length()

        def idx_step(i, p):
            cand = p | jnp.left_shift(jnp.int32(1), nbits - 1 - i)
            cnt = count_keys(lambda keys, kt: (keys == thr) & (key_pos(kt) < cand))
            return jnp.where(cnt < need, cand, p)
        cut = lax.fori_loop(0, nbits, idx_step, jnp.zeros((1, Q_BLOCK), I32))
        below = jnp.maximum(thr, INT_MIN + 1) - 1

        def demote(kt, carry):
            rows = key_rows(kt)
            keys = key_s[rows, :]
            key_s[rows, :] = jnp.where((keys == thr) & (key_pos(kt) > cut), below, keys)
            return carry
        lax.fori_loop(0, n_kt, demote, 0)

    thr_sel = jnp.where(thr == INT_MIN, INT_MIN + 1, thr)

    lanes4 = N_REP * Q_BLOCK
    q_f = q_ref[...].astype(F32).T
    row0 = lax.broadcasted_iota(I32, (LANES - ATT_HEAD_DIM, lanes4), 0) == 0
    ref_max = jnp.zeros((1, lanes4), F32)
    for n in range(ATT_KV_HEADS):
        blocks = [q_f[(n * N_REP + g) * ATT_HEAD_DIM:(n * N_REP + g + 1) * ATT_HEAD_DIM] for g in range(N_REP)]
        q_n = jnp.concatenate(blocks, axis=1)
        ref = (jnp.sqrt(jnp.sum(q_n * q_n, axis=0, keepdims=True) * kn_s[n:n + 1, 0:1]) * 1.01).astype(BF16)
        ref_max = jnp.maximum(ref_max, ref.astype(F32))
        extra = jnp.where(row0, -ref.astype(F32), 0.0)
        qt_s[n] = jnp.concatenate([q_n, extra], axis=0).astype(BF16)
    acc_s[...] = jnp.zeros(acc_s.shape, F32)
    fast = jnp.max(ref_max) <= FAST_BOUND

    @pl.when(fast)
    def _fixed_reference():
        def tile(kt, carry):
            rows = key_rows(kt)
            mask = jnp.where(key_s[rows, :] >= thr_sel, 1.0, 0.0).astype(BF16)
            mask4 = jnp.concatenate([mask] * N_REP, axis=1)
            def qk(n):
                return jnp.dot(k4_s[n, rows, :], qt_s[n], preferred_element_type=F32)

            def pv(n, s):
                p = jnp.exp2(s).astype(BF16) * mask4
                acc_s[n] += jnp.dot(vt_s[n * vrows:(n + 1) * vrows, rows], p, preferred_element_type=F32)

            s_cur = qk(0)
            for n in range(1, ATT_KV_HEADS):
                s_nxt = qk(n)
                pv(n - 1, s_cur)
                s_cur = s_nxt
            pv(ATT_KV_HEADS - 1, s_cur)
            return carry
        lax.fori_loop(0, n_kt, tile, 0)

    @pl.when(jnp.logical_not(fast))
    def _running_maximum():
        for n in range(ATT_KV_HEADS):
            qt_s[n, ATT_HEAD_DIM:, :] = jnp.zeros((LANES - ATT_HEAD_DIM, lanes4), BF16)
        m_s[...] = jnp.full(m_s.shape, MASK_VALUE, F32)

        def tile(kt, carry):
            rows = key_rows(kt)
            bias = jnp.where(key_s[rows, :] >= thr_sel, 0.0, MASK_VALUE)
            bias4 = jnp.concatenate([bias] * N_REP, axis=1)
            m_new = []
            for n in range(ATT_KV_HEADS):
                s = jnp.dot(k4_s[n, rows, :], qt_s[n], preferred_element_type=F32) + bias4
                s_s[n] = s
                tile_max = jnp.max(jnp.max(s.reshape(tk // 8, 8, lanes4), axis=0), axis=0, keepdims=True)
                m_new.append(jnp.maximum(m_s[n], tile_max))
            for n in range(ATT_KV_HEADS):
                alpha = jnp.exp2(m_s[n] - m_new[n])
                p = jnp.exp2(s_s[n] - m_new[n]).astype(BF16)
                acc_s[n] = alpha * acc_s[n] + jnp.dot(vt_s[n * vrows:(n + 1) * vrows, rows], p,
                                                      preferred_element_type=F32)
                m_s[n] = m_new[n]
            return carry
        lax.fori_loop(0, n_kt, tile, 0)

    heads = []
    for n in range(ATT_KV_HEADS):
        a = acc_s[n]
        o_n = a[:ATT_HEAD_DIM] / a[ATT_HEAD_DIM:ATT_HEAD_DIM + 1]
        heads += [o_n[:, g * Q_BLOCK:(g + 1) * Q_BLOCK] for g in range(N_REP)]
    o_ref[...] = jnp.concatenate(heads, axis=0).T.astype(o_ref.dtype)


def _dsa(pb, pf, bsz, seqlen, norm_g, norm_b):
    nb = seqlen // Q_BLOCK
    topk = min(TOPK_MAX, seqlen // 4)
    row = lambda b, i: b * nb + i
    const = lambda b, i: (0, 0)
    lanes4 = N_REP * Q_BLOCK
    vrows = ATT_HEAD_DIM + V_ONES_ROWS
    return pl.pallas_call(
        functools.partial(_dsa_kernel, seqlen=seqlen, topk=topk),
        out_shape=jax.ShapeDtypeStruct((bsz * seqlen, ATT_INNER), BF16),
        grid=(bsz, nb),
        in_specs=[
            pl.BlockSpec((Q_BLOCK, ATT_INNER), lambda b, i: (row(b, i), 0)),
            pl.BlockSpec((seqlen, ATT_KV_W), lambda b, i: (b, PB_K_BLK)),
            pl.BlockSpec((seqlen, ATT_KV_W), lambda b, i: (b, PB_V_BLK)),
            pl.BlockSpec((Q_BLOCK, IDX_HEADS * IDX_DIM), lambda b, i: (row(b, i), PB_QI_BLK)),
            pl.BlockSpec((seqlen, LANES), lambda b, i: (b, PF_IDX_BLK)),
            pl.BlockSpec((Q_BLOCK, LANES), lambda b, i: (row(b, i), PF_IDX_BLK)),
            pl.BlockSpec((1, IDX_DIM), const),
            pl.BlockSpec((1, IDX_DIM), const),
        ],
        out_specs=pl.BlockSpec((Q_BLOCK, ATT_INNER), lambda b, i: (row(b, i), 0)),
        scratch_shapes=[
            pltpu.VMEM((seqlen, IDX_DIM), BF16),
            pltpu.VMEM((ATT_KV_HEADS, seqlen, LANES), BF16),
            pltpu.VMEM((ATT_KV_HEADS * vrows, seqlen), BF16),
            pltpu.VMEM((seqlen, Q_BLOCK), I32),
            pltpu.VMEM((ATT_KV_HEADS, LANES, lanes4), BF16),
            pltpu.VMEM((ATT_KV_HEADS, vrows, lanes4), F32),
            pltpu.VMEM((ATT_KV_HEADS, 1, lanes4), F32),
            pltpu.VMEM((ATT_KV_HEADS, min(KEY_TILE, seqlen), lanes4), F32),
            pltpu.VMEM((8, LANES), F32),
        ],
        compiler_params=pltpu.CompilerParams(
            dimension_semantics=("parallel", "arbitrary"), vmem_limit_bytes=VMEM_LIMIT),
        name="dsa",
    )(pb, pb, pb, pb, pf, pf, norm_g.astype(F32).reshape(1, -1), norm_b.astype(F32).reshape(1, -1))


def _outproj_kernel(ys_ref, ya_ref, x_ref, w1_ref, w2_ref, g_ref, b_ref, h_ref):
    m = jnp.dot(ys_ref[...], w1_ref[...], preferred_element_type=F32)
    m = m + jnp.dot(ya_ref[...], w2_ref[...], preferred_element_type=F32)
    h_ref[...] = _layer_norm_rows(DEEPNORM_ALPHA * x_ref[...] + m, g_ref[...], b_ref[...])


def _out_proj(y_ssd, y_att, x2d, w_out, ln_g, ln_b, tm):
    n, d = x2d.shape
    w1 = w_out[:SSD_INNER].astype(BF16)
    w2 = w_out[SSD_INNER:].astype(BF16)
    rows = lambda i: (i, 0)
    const = lambda i: (0, 0)
    return pl.pallas_call(
        _outproj_kernel,
        out_shape=jax.ShapeDtypeStruct((n, d), F32),
        grid=(n // tm,),
        in_specs=[pl.BlockSpec((tm, SSD_INNER), rows), pl.BlockSpec((tm, ATT_INNER), rows),
                  pl.BlockSpec((tm, d), rows),
                  pl.BlockSpec((SSD_INNER, d), const), pl.BlockSpec((ATT_INNER, d), const),
                  pl.BlockSpec((1, d), const), pl.BlockSpec((1, d), const)],
        out_specs=pl.BlockSpec((tm, d), rows),
        compiler_params=pltpu.CompilerParams(
            dimension_semantics=("parallel",), vmem_limit_bytes=VMEM_LIMIT),
        name="out_proj_ln",
    )(y_ssd, y_att, x2d, w1, w2, ln_g.astype(F32).reshape(1, -1), ln_b.astype(F32).reshape(1, -1))


FFN_HALO = 16


def _ffn_kernel(h_ref, hprev_ref, wg_ref, wu_ref, cwg_ref, cwu_ref, cbg_ref, cbu_ref, wd_ref,
                g_ref, b_ref, o_ref, hb_ref, acc_ref, *, tiles_per_seq):
    i = pl.program_id(0)
    j = pl.program_id(1)
    tm = h_ref.shape[0]

    @pl.when(j == 0)
    def _stage():
        starts_sequence = (i % tiles_per_seq) == 0
        hb_ref[0:FFN_HALO, :] = jnp.where(starts_sequence, 0.0, hprev_ref[...]).astype(BF16)
        hb_ref[FFN_HALO:, :] = h_ref[...].astype(BF16)
        acc_ref[...] = jnp.zeros_like(acc_ref)

    hb = hb_ref[...]

    def conv_branch(w_ref, cw_ref, cb_ref):
        u = jnp.dot(hb, w_ref[...], preferred_element_type=F32)
        c = cb_ref[...] + cw_ref[FFN_CONV - 1:FFN_CONV, :] * u
        for back in range(1, FFN_CONV):
            c = c + cw_ref[FFN_CONV - 1 - back:FFN_CONV - back, :] * pltpu.roll(u, back, 0)
        return c[FFN_HALO:]

    act = _silu(conv_branch(wg_ref, cwg_ref, cbg_ref)) * conv_branch(wu_ref, cwu_ref, cbu_ref)
    acc_ref[...] += jnp.dot(act.astype(BF16), wd_ref[...], preferred_element_type=F32)

    @pl.when(j == pl.num_programs(1) - 1)
    def _finish():
        o_ref[...] = _layer_norm_rows(DEEPNORM_ALPHA * h_ref[...] + acc_ref[...], g_ref[...], b_ref[...])


def _conv_ffn(h, seqlen, w_up, conv_w, conv_b, w_down, ln_g, ln_b, tm, n_split):
    n, d = h.shape
    d_ff = w_down.shape[0]
    tf = d_ff // n_split
    halo_blocks = tm // FFN_HALO
    w_up_b = w_up.astype(BF16)
    conv_b2 = conv_b.astype(F32).reshape(1, -1)
    rows = lambda i, j: (i, 0)
    const = lambda i, j: (0, 0)
    gate = lambda i, j: (0, j)
    up = lambda i, j: (0, n_split + j)
    return pl.pallas_call(
        functools.partial(_ffn_kernel, tiles_per_seq=seqlen // tm),
        out_shape=jax.ShapeDtypeStruct((n, d), F32),
        grid=(n // tm, n_split),
        in_specs=[pl.BlockSpec((tm, d), rows),
                  pl.BlockSpec((FFN_HALO, d), lambda i, j: (jnp.maximum(i * halo_blocks - 1, 0), 0)),
                  pl.BlockSpec((d, tf), gate), pl.BlockSpec((d, tf), up),
                  pl.BlockSpec((FFN_CONV, tf), gate), pl.BlockSpec((FFN_CONV, tf), up),
                  pl.BlockSpec((1, tf), gate), pl.BlockSpec((1, tf), up),
                  pl.BlockSpec((tf, d), lambda i, j: (j, 0)),
                  pl.BlockSpec((1, d), const), pl.BlockSpec((1, d), const)],
        out_specs=pl.BlockSpec((tm, d), rows),
        scratch_shapes=[pltpu.VMEM((tm + FFN_HALO, d), BF16), pltpu.VMEM((tm, d), F32)],
        compiler_params=pltpu.CompilerParams(
            dimension_semantics=("parallel", "arbitrary"), vmem_limit_bytes=VMEM_LIMIT),
        name="conv_ffn_ln",
    )(h, h, w_up_b, w_up_b, conv_w.astype(F32), conv_w.astype(F32), conv_b2, conv_b2,
      w_down.astype(BF16), ln_g.astype(F32).reshape(1, -1), ln_b.astype(F32).reshape(1, -1))


def _split_w_in(w_in):
    o = 0
    parts = {}
    for name, width in (("z", SSD_INNER), ("xbc", SSD_CONV_DIM), ("dt", SSD_HEADS), ("q", ATT_INNER),
                        ("k", ATT_KV_W), ("v", ATT_KV_W), ("qi", IDX_HEADS * IDX_DIM),
                        ("ki", IDX_DIM), ("wi", IDX_HEADS)):
        parts[name] = w_in[:, o:o + width]
        o += width
    d = w_in.shape[0]
    zeros = lambda w: jnp.zeros((d, w), w_in.dtype)
    wf = jnp.concatenate([parts["xbc"], parts["z"], parts["dt"], zeros(LANES - SSD_HEADS),
                          parts["ki"], parts["wi"], zeros(LANES - IDX_DIM - IDX_HEADS)], axis=1)
    q_scaled = parts["q"] * (ATT_HEAD_DIM ** -0.5 * LOG2E)
    wb = jnp.concatenate([q_scaled, parts["k"], parts["v"], parts["qi"]], axis=1)
    return wf.astype(BF16), wb.astype(BF16)


def _row_tile(n, seqlen, want):
    tm = min(want, seqlen)
    assert n % tm == 0 and seqlen % tm == 0
    return tm


def kernel(x, w_in, ssd_conv_w, ssd_conv_b, dt_bias, a_log, d_skip, ssd_norm_g, idx_k_norm_g,
           idx_k_norm_b, w_out, ln1_g, ln1_b, ffn_w_up, ffn_conv_w, ffn_conv_b, ffn_w_down, ln2_g, ln2_b):
    bsz, seqlen, d = x.shape
    n = bsz * seqlen
    assert seqlen % SSD_CHUNK == 0 and seqlen % Q_BLOCK == 0 and seqlen % min(KEY_TILE, seqlen) == 0
    h = x.reshape(n, d)
    for i in range(DEPTH):
        wf, wb = _split_w_in(w_in[i])
        tm = _row_tile(n, seqlen, 1024)
        pf = _project(h, wf, F32, tm, PF_WIDTH // 2)
        pb = _project(h, wb, BF16, tm, PB_WIDTH // 2)
        y_ssd = _ssd(pf, bsz, seqlen, ssd_conv_w[i], ssd_conv_b[i], dt_bias[i], a_log[i], d_skip[i],
                     ssd_norm_g[i])
        y_att = _dsa(pb, pf, bsz, seqlen, idx_k_norm_g[i], idx_k_norm_b[i])
        h1 = _out_proj(y_ssd, y_att, h, w_out[i], ln1_g[i], ln1_b[i], _row_tile(n, seqlen, 512))
        h = _conv_ffn(h1, seqlen, ffn_w_up[i], ffn_conv_w[i], ffn_conv_b[i], ffn_w_down[i],
                      ln2_g[i], ln2_b[i], _row_tile(n, seqlen, 512), 2)
    return h.reshape(bsz, seqlen, d)
```
